```python
import jax, jax.numpy as jnp
from jax import lax
import numpy as np

D_MODEL = 1024
BATCH = 8
SEQ = 8192
DEPTH = 2

GRID_W = 64
CTX_LEN = 256
N_HEADS = 8
N_KV_HEADS = 2
GROUP = N_HEADS // N_KV_HEADS
HEAD_DIM = 64
ROPE_AXIS_DIM = HEAD_DIM // 2
ROPE_THETA = 10000.0
WINDOW = 128
ATTN_BLOCK = 128
D_CONV = D_MODEL // 2
CONV_K = 31
D_POOL = D_MODEL // 2
POOL_WINDOWS = (2, 4, 8, 16)
POOL_GROUP = D_POOL // len(POOL_WINDOWS)
N_BRANCH = 3
D_Q = N_HEADS * HEAD_DIM
D_KV = N_KV_HEADS * HEAD_DIM
IN_SPLITS = (D_CONV, D_CONV, D_Q, D_KV, D_KV, D_POOL, N_BRANCH * D_MODEL)
IN_OFFSETS = tuple(sum(IN_SPLITS[: i + 1]) for i in range(len(IN_SPLITS) - 1))
D_IN = sum(IN_SPLITS)
N_EXPERTS = 32
TOP_K = 4
D_EXPERT = D_MODEL
SWIGLU_LIMIT = 7.0
SWIGLU_ALPHA = 1.702
MOE_BLOCK = 128
EPS = 1e-6
MASK_VALUE = -1e30

kernel_name = "hybrid_gated_conv_swa_pool_moe_dit"


def rmsnorm(x, g):
    xf = x.astype(jnp.float32)
    y = xf * lax.rsqrt(jnp.mean(jnp.square(xf), axis=-1, keepdims=True) + EPS)
    return y.astype(x.dtype) * g


def layernorm(x, g, b):
    xf = x.astype(jnp.float32)
    mu = jnp.mean(xf, axis=-1, keepdims=True)
    var = jnp.mean(jnp.square(xf - mu), axis=-1, keepdims=True)
    return ((xf - mu) * lax.rsqrt(var + EPS)).astype(x.dtype) * g + b


def modulate(h, shift, scale):
    return h * (1 + scale) + shift


def axial_rope_tables(rows):
    row = jnp.repeat(jnp.arange(rows), GRID_W).astype(jnp.float32)
    col = jnp.tile(jnp.arange(GRID_W), rows).astype(jnp.float32)
    inv = ROPE_THETA ** (-jnp.arange(0, ROPE_AXIS_DIM, 2, dtype=jnp.float32) / ROPE_AXIS_DIM)
    ang = jnp.concatenate([row[:, None] * inv, col[:, None] * inv], axis=-1)
    return jnp.cos(ang), jnp.sin(ang)


def apply_rope(x, cos, sin):
    half = HEAD_DIM // 2
    x1, x2 = x[..., :half], x[..., half:]
    c = cos[None, :, None, :].astype(x.dtype)
    s = sin[None, :, None, :].astype(x.dtype)
    return jnp.concatenate([x1 * c - x2 * s, x2 * c + x1 * s], axis=-1)


def sink_softmax(logits, sink):
    s = jnp.broadcast_to(sink.reshape(N_KV_HEADS, GROUP, 1, 1).astype(jnp.float32), logits.shape[:-1] + (1,))
    p = jax.nn.softmax(jnp.concatenate([logits, s], axis=-1), axis=-1)
    return p[..., :-1]


def latent_attention(q, k, v, kz, vz, sink):
    B, S = q.shape[0], q.shape[1]
    nb = S // ATTN_BLOCK
    scale = HEAD_DIM ** -0.5
    qb = q.reshape(B, nb, ATTN_BLOCK, N_KV_HEADS, GROUP, HEAD_DIM)
    pad = ((0, 0), (ATTN_BLOCK, ATTN_BLOCK), (0, 0), (0, 0))
    kp = jnp.pad(k, pad).reshape(B, nb + 2, ATTN_BLOCK, N_KV_HEADS, HEAD_DIM)
    vp = jnp.pad(v, pad).reshape(B, nb + 2, ATTN_BLOCK, N_KV_HEADS, HEAD_DIM)
    kw = jnp.concatenate([kp[:, :-2], kp[:, 1:-1], kp[:, 2:]], axis=2)
    vw = jnp.concatenate([vp[:, :-2], vp[:, 1:-1], vp[:, 2:]], axis=2)
    s_loc = jnp.einsum('bnqkgd,bnjkd->bnkgqj', qb, kw).astype(jnp.float32) * scale
    qi = jnp.arange(ATTN_BLOCK)[:, None]
    kj = jnp.arange(3 * ATTN_BLOCK)[None, :]
    in_window = jnp.abs(kj - ATTN_BLOCK - qi) <= WINDOW
    kpos = jnp.arange(nb)[:, None] * ATTN_BLOCK - ATTN_BLOCK + kj
    in_range = (kpos >= 0) & (kpos < S)
    valid = in_window[None] & in_range[:, None, :]
    s_loc = jnp.where(valid[None, :, None, None], s_loc, MASK_VALUE)
    s_ctx = jnp.einsum('bnqkgd,bckd->bnkgqc', qb, kz).astype(jnp.float32) * scale
    p = sink_softmax(jnp.concatenate([s_loc, s_ctx], axis=-1), sink)
    n_loc = 3 * ATTN_BLOCK
    p_loc = p[..., :n_loc].astype(v.dtype)
    p_ctx = p[..., n_loc:].astype(v.dtype)
    out = jnp.einsum('bnkgqj,bnjkd->bnqkgd', p_loc, vw) + jnp.einsum('bnkgqc,bckd->bnqkgd', p_ctx, vz)
    return out.reshape(B, S, D_Q)


def context_attention(qz, kz, vz, sink):
    B, C = qz.shape[0], qz.shape[1]
    qg = qz.reshape(B, C, N_KV_HEADS, GROUP, HEAD_DIM)
    s = jnp.einsum('bqkgd,bckd->bkgqc', qg, kz).astype(jnp.float32) * (HEAD_DIM ** -0.5)
    p = sink_softmax(s, sink).astype(vz.dtype)
    return jnp.einsum('bkgqc,bckd->bqkgd', p, vz).reshape(B, C, D_Q)


def conv_mix(a_val, a_gate, w_dw, b_dw, ln_g, ln_b, w_conv_out):
    a = a_val * jax.nn.sigmoid(a_gate)
    a = lax.conv_general_dilated(
        a, w_dw[:, None, :], window_strides=(1,), padding=((CONV_K // 2, CONV_K // 2),),
        dimension_numbers=('NWC', 'WIO', 'NWC'), feature_group_count=D_CONV) + b_dw
    a = jax.nn.silu(layernorm(a, ln_g, ln_b))
    return a @ w_conv_out


def pool_mix(u, w_pool, pool_scale, w_pool_out):
    B, L, _ = u.shape
    cs = jnp.pad(jnp.cumsum(u.astype(jnp.float32), axis=1), ((0, 0), (1, 0), (0, 0)))
    t = jnp.arange(L)
    outs = []
    for gi, w in enumerate(POOL_WINDOWS):
        lo = jnp.clip(t - w // 2, 0, L)
        hi = jnp.clip(t + w - w // 2, 0, L)
        sl = slice(gi * POOL_GROUP, (gi + 1) * POOL_GROUP)
        cnt = (hi - lo).astype(jnp.float32)[None, :, None]
        outs.append((cs[:, hi, sl] - cs[:, lo, sl]) / cnt)
    pooled = jnp.concatenate(outs, axis=-1).astype(u.dtype) - u
    pooled = pooled.reshape(B, L, len(POOL_WINDOWS), POOL_GROUP)
    mixed = jnp.einsum('blgc,gcd->blgd', pooled, w_pool).reshape(B, L, D_POOL) * pool_scale
    return mixed @ w_pool_out


def merge_branches(y_conv, y_attn, y_pool, gate_logits, b_gate, w_out):
    g_conv, g_attn, g_pool = jnp.split(jax.nn.sigmoid(gate_logits + b_gate), N_BRANCH, axis=-1)
    return (g_conv * y_conv + g_attn * y_attn + g_pool * y_pool) @ w_out


def routed_ffn(h, router_w, router_b, w_gu, b_gu, w_down, b_down):
    T, D = h.shape
    logits = (h @ router_w + router_b).astype(jnp.float32)
    top_v, top_i = lax.top_k(logits, TOP_K)
    gates = jax.nn.softmax(top_v, axis=-1).astype(h.dtype)
    A = T * TOP_K
    flat_e = top_i.reshape(-1)
    flat_tok = jnp.repeat(jnp.arange(T), TOP_K)
    order = jnp.argsort(flat_e)
    sorted_e = flat_e[order]
    sorted_tok = flat_tok[order]
    sorted_gate = gates.reshape(-1)[order]
    counts = jnp.bincount(flat_e, length=N_EXPERTS)
    padded = (counts + MOE_BLOCK - 1) // MOE_BLOCK * MOE_BLOCK
    start = jnp.cumsum(counts) - counts
    padded_end = jnp.cumsum(padded)
    padded_start = padded_end - padded
    dest = padded_start[sorted_e] + (jnp.arange(A) - start[sorted_e])
    n_blk = -(-A // MOE_BLOCK) + N_EXPERTS
    P = n_blk * MOE_BLOCK
    x_pad = jnp.zeros((P, D), h.dtype).at[dest].set(h[sorted_tok])
    block_e = jnp.minimum(jnp.searchsorted(padded_end, jnp.arange(n_blk) * MOE_BLOCK, side='right'), N_EXPERTS - 1)

    def expert_block(args):
        xb, e = args
        gu = xb @ w_gu[e] + b_gu[e]
        g, up = gu[:, :D_EXPERT], gu[:, D_EXPERT:]
        g = jnp.minimum(g, SWIGLU_LIMIT)
        up = jnp.clip(up, -SWIGLU_LIMIT, SWIGLU_LIMIT)
        act = (up + 1) * (g * jax.nn.sigmoid(SWIGLU_ALPHA * g))
        return act @ w_down[e] + b_down[e]

    y = lax.map(expert_block, (x_pad.reshape(n_blk, MOE_BLOCK, D), block_e)).reshape(P, D)
    return jnp.zeros((T, D), h.dtype).at[sorted_tok].add(y[dest] * sorted_gate[:, None])


def trunk_layer(x, z, c, c_ctx, cos, sin, w_mod, b_mod, norm1_g, w_in, b_gate, w_dw, b_dw, conv_ln_g, conv_ln_b,
                w_conv_out, attn_sink, w_attn_out, w_pool, pool_scale, w_pool_out, w_out, norm2_g,
                router_w, router_b, w_gu, b_gu, w_down, b_down, last):
    B, S, D = x.shape
    C = z.shape[1]
    sh1, sc1, g1, sh2, sc2, g2 = jnp.split((jax.nn.silu(c) @ w_mod + b_mod)[:, None, :], 6, axis=-1)
    zsh1, zsc1, zg1, zsh2, zsc2, zg2 = jnp.split(jax.nn.silu(c_ctx) @ w_mod + b_mod, 6, axis=-1)

    h = modulate(rmsnorm(x, norm1_g), sh1, sc1)
    hz = modulate(rmsnorm(z, norm1_g), zsh1, zsc1)
    a_val, a_gate, q, k, v, u, gate_logits = jnp.split(h @ w_in, IN_OFFSETS, axis=-1)
    if last:
        zk, zv = jnp.split(hz @ w_in[:, IN_OFFSETS[2]:IN_OFFSETS[4]], 2, axis=-1)
    else:
        za_val, za_gate, zq, zk, zv, zu, zgate_logits = jnp.split(hz @ w_in, IN_OFFSETS, axis=-1)
    zk = zk.reshape(B, C, N_KV_HEADS, HEAD_DIM)
    zv = zv.reshape(B, C, N_KV_HEADS, HEAD_DIM)
    q = apply_rope(q.reshape(B, S, N_HEADS, HEAD_DIM), cos, sin)
    k = apply_rope(k.reshape(B, S, N_KV_HEADS, HEAD_DIM), cos, sin)
    v = v.reshape(B, S, N_KV_HEADS, HEAD_DIM)

    y_conv = conv_mix(a_val, a_gate, w_dw, b_dw, conv_ln_g, conv_ln_b, w_conv_out)
    y_attn = latent_attention(q, k, v, zk, zv, attn_sink) @ w_attn_out
    y_pool = pool_mix(u, w_pool, pool_scale, w_pool_out)
    x = x + g1 * merge_branches(y_conv, y_attn, y_pool, gate_logits, b_gate, w_out)

    if not last:
        zy_conv = conv_mix(za_val, za_gate, w_dw, b_dw, conv_ln_g, conv_ln_b, w_conv_out)
        zy_attn = context_attention(zq.reshape(B, C, N_HEADS, HEAD_DIM), zk, zv, attn_sink) @ w_attn_out
        zy_pool = pool_mix(zu, w_pool, pool_scale, w_pool_out)
        z = z + zg1 * merge_branches(zy_conv, zy_attn, zy_pool, zgate_logits, b_gate, w_out)

    h2 = modulate(rmsnorm(x, norm2_g), sh2, sc2)
    if last:
        f = routed_ffn(h2.reshape(B * S, D), router_w, router_b, w_gu, b_gu, w_down, b_down)
        x = x + g2 * f.reshape(B, S, D)
    else:
        h2z = modulate(rmsnorm(z, norm2_g), zsh2, zsc2)
        f = routed_ffn(jnp.concatenate([h2.reshape(B * S, D), h2z.reshape(B * C, D)], axis=0),
                       router_w, router_b, w_gu, b_gu, w_down, b_down)
        x = x + g2 * f[: B * S].reshape(B, S, D)
        z = z + zg2 * f[B * S:].reshape(B, C, D)
    return x, z


def setup_inputs(seed: int = 0) -> dict:
    key = jax.random.key(seed)
    ks = iter(jax.random.split(key, 40))
    L, D, E, F = DEPTH, D_MODEL, N_EXPERTS, D_EXPERT

    def nrm(shape, scale):
        return jax.random.normal(next(ks), shape, jnp.float32) * scale

    return {
        "x": nrm((BATCH, SEQ, D), 1.0),
        "c": nrm((BATCH, D), 1.0),
        "ctx": nrm((BATCH, CTX_LEN, D), 1.0),
        "c_ctx": nrm((D,), 1.0),
        "w_mod": nrm((L, D, 6 * D), 0.5 * D ** -0.5),
        "b_mod": nrm((L, 6 * D), 0.02),
        "norm1_g": 1.0 + nrm((L, D), 0.02),
        "w_in": nrm((L, D, D_IN), D ** -0.5),
        "b_gate": nrm((L, N_BRANCH * D), 0.02),
        "w_dw": nrm((L, CONV_K, D_CONV), CONV_K ** -0.5),
        "b_dw": nrm((L, D_CONV), 0.02),
        "conv_ln_g": 1.0 + nrm((L, D_CONV), 0.02),
        "conv_ln_b": nrm((L, D_CONV), 0.02),
        "w_conv_out": nrm((L, D_CONV, D), D_CONV ** -0.5),
        "attn_sink": nrm((L, N_HEADS), 0.5),
        "w_attn_out": nrm((L, D_Q, D), D_Q ** -0.5),
        "w_pool": nrm((L, len(POOL_WINDOWS), POOL_GROUP, POOL_GROUP), POOL_GROUP ** -0.5),
        "pool_scale": 1.0 + nrm((L, D_POOL), 0.1),
        "w_pool_out": nrm((L, D_POOL, D), D_POOL ** -0.5),
        "w_out": nrm((L, D, D), D ** -0.5),
        "norm2_g": 1.0 + nrm((L, D), 0.02),
        "router_w": nrm((L, D, E), D ** -0.5),
        "router_b": nrm((L, E), 0.01),
        "w_gu": nrm((L, E, D, 2 * F), D ** -0.5),
        "b_gu": nrm((L, E, 2 * F), 0.02),
        "w_down": nrm((L, E, F, D), F ** -0.5),
        "b_down": nrm((L, E, D), 0.02),
        "final_g": 1.0 + nrm((D,), 0.02),
    }


def reference(x, c, ctx, c_ctx, w_mod, b_mod, norm1_g, w_in, b_gate, w_dw, b_dw, conv_ln_g, conv_ln_b,
              w_conv_out, attn_sink, w_attn_out, w_pool, pool_scale, w_pool_out, w_out, norm2_g,
              router_w, router_b, w_gu, b_gu, w_down, b_down, final_g):
    rows = x.shape[1] // GRID_W
    cos, sin = axial_rope_tables(rows)
    z = ctx
    for l in range(DEPTH):
        x, z = trunk_layer(
            x, z, c, c_ctx, cos, sin, w_mod[l], b_mod[l], norm1_g[l], w_in[l], b_gate[l], w_dw[l], b_dw[l],
            conv_ln_g[l], conv_ln_b[l], w_conv_out[l], attn_sink[l], w_attn_out[l], w_pool[l], pool_scale[l],
            w_pool_out[l], w_out[l], norm2_g[l], router_w[l], router_b[l], w_gu[l], b_gu[l], w_down[l], b_down[l],
            last=(l == DEPTH - 1))
    return rmsnorm(x, final_g)
```

```python
import functools

import jax
import jax.numpy as jnp
from jax import lax
from jax.experimental import pallas as pl
from jax.experimental.pallas import tpu as pltpu

F32 = jnp.float32
BF16 = jnp.bfloat16

D_MODEL = 1024
GRID_W = 64
N_HEADS = 8
N_KV_HEADS = 2
GROUP = N_HEADS // N_KV_HEADS
HEAD_DIM = 64
ROPE_AXIS_DIM = HEAD_DIM // 2
ROPE_THETA = 10000.0
WINDOW = 128
ATTN_BLOCK = 128
D_CONV = 512
CONV_K = 31
D_POOL = 512
POOL_WINDOWS = (2, 4, 8, 16)
POOL_GROUP = D_POOL // len(POOL_WINDOWS)
N_BRANCH = 3
D_Q = N_HEADS * HEAD_DIM
D_KV = N_KV_HEADS * HEAD_DIM
N_EXPERTS = 32
TOP_K = 4
D_EXPERT = 1024
SWIGLU_LIMIT = 7.0
SWIGLU_ALPHA = 1.702
EPS = 1e-6
MASK_VALUE = -1e30

O_AVAL, O_AGATE, O_Q, O_K, O_V, O_U, O_GATES = 0, 512, 1024, 1536, 1664, 1792, 2304
D_IN = O_GATES + N_BRANCH * D_MODEL

HALO = 16
MOE_ROWS = 256
COMBINE_ROWS = 128
VMEM_LIMIT = 56 * 1024 * 1024


def _cparams(*sem):
    return pltpu.CompilerParams(dimension_semantics=sem, vmem_limit_bytes=VMEM_LIMIT)


def _sigmoid(x):
    return 1.0 / (1.0 + jnp.exp(-x))


def _rms(x, g):
    return x * lax.rsqrt(jnp.mean(x * x, axis=-1, keepdims=True) + EPS) * g


def _mod_kernel(c_ref, w_ref, b_ref, o_ref):
    c = c_ref[...]
    s = c * _sigmoid(c)
    o_ref[...] = jnp.dot(s.astype(BF16), w_ref[...].astype(BF16), preferred_element_type=F32) + b_ref[...]


def _modulation(c_rows, w_mod, b_mod):
    n = w_mod.shape[1]
    tn = 1536
    return pl.pallas_call(
        _mod_kernel,
        grid=(n // tn,),
        in_specs=[pl.BlockSpec((16, D_MODEL), lambda j: (0, 0)),
                  pl.BlockSpec((D_MODEL, tn), lambda j: (0, j)),
                  pl.BlockSpec((1, tn), lambda j: (0, j))],
        out_specs=pl.BlockSpec((16, tn), lambda j: (0, j)),
        out_shape=jax.ShapeDtypeStruct((16, n), F32),
        compiler_params=_cparams("arbitrary"),
        name="modulation",
    )(c_rows, w_mod, b_mod.reshape(1, n))


def _rope(t, cos, sin):
    w = t.shape[1]
    reps = w // cos.shape[1]
    cs = jnp.concatenate([cos] * reps, axis=1) if reps > 1 else cos
    sn = jnp.concatenate([sin] * reps, axis=1) if reps > 1 else sin
    lane = lax.broadcasted_iota(jnp.int32, t.shape, 1)
    first_half = (lane % HEAD_DIM) < (HEAD_DIM // 2)
    partner = jnp.where(first_half, pltpu.roll(t, w - HEAD_DIM // 2, 1), pltpu.roll(t, HEAD_DIM // 2, 1))
    return t * cs + partner * sn


def _inproj_kernel(x_ref, sh_ref, sc_ref, g_ref, w_ref, bg_ref, cos_ref, sin_ref,
                   a_ref, q_ref, k_ref, v_ref, u_ref, gt_ref):
    x = x_ref[0]
    h = _rms(x, g_ref[...]) * (1.0 + sc_ref[0]) + sh_ref[0]
    hb = h.astype(BF16)

    def mm(lo, hi):
        return jnp.dot(hb, w_ref[:, lo:hi], preferred_element_type=F32)

    ag = mm(O_AVAL, O_Q)
    a_ref[0] = ag[:, :D_CONV] * _sigmoid(ag[:, D_CONV:])
    cos = cos_ref[...]
    sin = sin_ref[...]
    q = mm(O_Q, O_K)
    q_ref[0] = (_rope(q, cos, sin) * (HEAD_DIM ** -0.5)).astype(BF16)
    kv = mm(O_K, O_U)
    k_ref[0] = _rope(kv[:, :D_KV], cos, sin).astype(BF16)
    v_ref[0] = kv[:, D_KV:].astype(BF16)
    u_ref[0] = mm(O_U, O_GATES)
    for j in range(N_BRANCH):
        lo = O_GATES + j * D_MODEL
        gl = mm(lo, lo + D_MODEL) + bg_ref[:, j * D_MODEL:(j + 1) * D_MODEL]
        gt_ref[0, :, j * D_MODEL:(j + 1) * D_MODEL] = _sigmoid(gl).astype(BF16)


def _inproj(x, shift, scale, norm_g, w_in_bf, b_gate, cos_t, sin_t):
    nb, L, _ = x.shape
    tm = min(512, L)
    full = lambda shp: pl.BlockSpec(shp, lambda b, i: (0,) * len(shp))
    tile = lambda w: pl.BlockSpec((1, tm, w), lambda b, i: (b, i, 0))
    per_b = pl.BlockSpec((1, 1, D_MODEL), lambda b, i: (b, 0, 0))
    outs = [(D_CONV, F32), (D_Q, BF16), (D_KV, BF16), (D_KV, BF16), (D_POOL, F32), (N_BRANCH * D_MODEL, BF16)]
    return pl.pallas_call(
        _inproj_kernel,
        grid=(nb, L // tm),
        in_specs=[tile(D_MODEL), per_b, per_b, full((1, D_MODEL)),
                  pl.BlockSpec((D_MODEL, D_IN), lambda b, i: (0, 0), pipeline_mode=pl.Buffered(1)),
                  full((1, N_BRANCH * D_MODEL)),
                  pl.BlockSpec((tm, 2 * HEAD_DIM), lambda b, i: (i, 0)),
                  pl.BlockSpec((tm, 2 * HEAD_DIM), lambda b, i: (i, 0))],
        out_specs=[tile(w) for w, _ in outs],
        out_shape=[jax.ShapeDtypeStruct((nb, L, w), dt) for w, dt in outs],
        compiler_params=_cparams("parallel", "arbitrary"),
        name="inproj",
    )(x, shift, scale, norm_g.reshape(1, D_MODEL), w_in_bf, b_gate.reshape(1, -1), cos_t, sin_t)


def _attn_kernel(sink_ref, q_ref, *refs, seq_len, local):
    if local:
        kp_ref, kc_ref, kn_ref, vp_ref, vc_ref, vn_ref, zk_ref, zv_ref, o_ref = refs
    else:
        zk_ref, zv_ref, o_ref = refs
    i = pl.program_id(1)
    q = q_ref[0].astype(F32)
    rows = q.shape[0]
    if local:
        k_all = jnp.concatenate([kp_ref[0], kc_ref[0], kn_ref[0], zk_ref[0]], axis=0).astype(F32)
        v_all = jnp.concatenate([vp_ref[0], vc_ref[0], vn_ref[0], zv_ref[0]], axis=0).astype(F32)
        nk = k_all.shape[0]
        qi = lax.broadcasted_iota(jnp.int32, (rows, nk), 0)
        kj = lax.broadcasted_iota(jnp.int32, (rows, nk), 1)
        kpos = i * ATTN_BLOCK - ATTN_BLOCK + kj
        in_band = jnp.where(jnp.abs(kj - ATTN_BLOCK - qi) <= WINDOW, 1, 0)
        in_range = jnp.where(kpos >= 0, jnp.where(kpos < seq_len, 1, 0), 0)
        is_ctx = jnp.where(kj >= 3 * ATTN_BLOCK, 1, 0)
        valid = (in_band * in_range + is_ctx) > 0
        valid = jnp.concatenate([valid] * GROUP, axis=0)
    else:
        k_all = zk_ref[0].astype(F32)
        v_all = zv_ref[0].astype(F32)
    outs = []
    for g in range(N_KV_HEADS):
        kg = k_all[:, g * HEAD_DIM:(g + 1) * HEAD_DIM].astype(BF16)
        vg = v_all[:, g * HEAD_DIM:(g + 1) * HEAD_DIM].astype(BF16)
        heads = [g * GROUP + h for h in range(GROUP)]
        qg = jnp.concatenate([q[:, hh * HEAD_DIM:(hh + 1) * HEAD_DIM] for hh in heads], axis=0).astype(BF16)
        s = lax.dot_general(qg, kg, (((1,), (1,)), ((), ())), preferred_element_type=F32)
        if local:
            s = jnp.where(valid, s, MASK_VALUE)
        sink = jnp.concatenate([jnp.full((rows, 1), sink_ref[hh], F32) for hh in heads], axis=0)
        m = jnp.maximum(jnp.max(s, axis=-1, keepdims=True), sink)
        p = jnp.exp(s - m)
        den = jnp.sum(p, axis=-1, keepdims=True) + jnp.exp(sink - m)
        o = jnp.dot(p.astype(BF16), vg, preferred_element_type=F32) / den
        outs += [o[h * rows:(h + 1) * rows] for h in range(GROUP)]
    o_ref[0] = jnp.concatenate(outs, axis=1).astype(BF16)


def _attention(q, k, v, zk, zv, sink, local):
    nb, L, _ = q.shape
    C = zk.shape[1]
    tq = ATTN_BLOCK if local else L
    nblk = L // tq
    qspec = pl.BlockSpec((1, tq, D_Q), lambda b, i: (b, i, 0))
    zspec = pl.BlockSpec((1, C, D_KV), lambda b, i: (b, 0, 0))
    if local:
        prev = pl.BlockSpec((1, tq, D_KV), lambda b, i: (b, jnp.maximum(i - 1, 0), 0))
        cur = pl.BlockSpec((1, tq, D_KV), lambda b, i: (b, i, 0))
        nxt = pl.BlockSpec((1, tq, D_KV), lambda b, i: (b, jnp.minimum(i + 1, nblk - 1), 0))
        in_specs = [qspec, prev, cur, nxt, prev, cur, nxt, zspec, zspec]
        args = (q, k, k, k, v, v, v, zk, zv)
    else:
        in_specs = [qspec, zspec, zspec]
        args = (q, zk, zv)
    return pl.pallas_call(
        functools.partial(_attn_kernel, seq_len=L, local=local),
        grid=(nb, nblk),
        in_specs=[pl.BlockSpec(memory_space=pltpu.SMEM)] + in_specs,
        out_specs=qspec,
        out_shape=jax.ShapeDtypeStruct((nb, L, D_Q), BF16),
        compiler_params=_cparams("parallel", "arbitrary"),
        name="attention_local" if local else "attention_ctx",
    )(sink, *args)


def _mix_kernel(x_ref, a_ref, ap_ref, an_ref, at_ref, u_ref, up_ref, un_ref, gt_ref, g1_ref, sh2_ref, sc2_ref,
                wdw_ref, bdw_ref, lng_ref, lnb_ref, wco_ref, wao_ref, wpool_ref, psc_ref, wpo_ref, wout_ref,
                n2g_ref, rw_ref, rb_ref,
                x1_ref, h2_ref, ti_ref, tg_ref, ext_ref, *, seq_len):
    i = pl.program_id(1)
    nt = pl.num_programs(1)
    tm = x_ref.shape[1]
    has_prev = (i > 0).astype(F32)
    has_next = (i < nt - 1).astype(F32)

    def fill_ext(c_ref, p_ref, n_ref):
        ext_ref[0:HALO, :] = p_ref[0] * has_prev
        ext_ref[HALO:HALO + tm, :] = c_ref[0]
        ext_ref[HALO + tm:HALO + tm + HALO, :] = n_ref[0] * has_next

    fill_ext(a_ref, ap_ref, an_ref)
    acc = jnp.zeros((tm, D_CONV), F32) + bdw_ref[...]
    for j in range(CONV_K):
        acc = acc + ext_ref[pl.ds(HALO - CONV_K // 2 + j, tm), :] * wdw_ref[j:j + 1, :]
    mu = jnp.mean(acc, axis=-1, keepdims=True)
    cen = acc - mu
    var = jnp.mean(cen * cen, axis=-1, keepdims=True)
    ln = cen * lax.rsqrt(var + EPS) * lng_ref[...] + lnb_ref[...]
    act = ln * _sigmoid(ln)
    y_conv = jnp.dot(act.astype(BF16), wco_ref[...], preferred_element_type=F32)

    fill_ext(u_ref, up_ref, un_ref)
    t = i * tm + lax.broadcasted_iota(jnp.int32, (tm, 1), 0)
    u = u_ref[0]
    mixed = []
    for gi, w in enumerate(POOL_WINDOWS):
        sl = slice(gi * POOL_GROUP, (gi + 1) * POOL_GROUP)
        tot = jnp.zeros((tm, POOL_GROUP), F32)
        for d in range(-(w // 2), w - w // 2):
            tot = tot + ext_ref[pl.ds(HALO + d, tm), sl]
        cnt = (jnp.minimum(t + (w - w // 2), seq_len) - jnp.maximum(t - w // 2, 0)).astype(F32)
        pooled = tot / cnt - u[:, sl]
        mixed.append(jnp.dot(pooled.astype(BF16), wpool_ref[gi], preferred_element_type=F32))
    mixed = jnp.concatenate(mixed, axis=1) * psc_ref[...]
    y_pool = jnp.dot(mixed.astype(BF16), wpo_ref[...], preferred_element_type=F32)

    y_attn = jnp.dot(at_ref[0], wao_ref[...], preferred_element_type=F32)

    merged = (gt_ref[0, :, 0:D_MODEL].astype(F32) * y_conv
              + gt_ref[0, :, D_MODEL:2 * D_MODEL].astype(F32) * y_attn
              + gt_ref[0, :, 2 * D_MODEL:3 * D_MODEL].astype(F32) * y_pool)
    x1 = x_ref[0] + g1_ref[0] * jnp.dot(merged.astype(BF16), wout_ref[...], preferred_element_type=F32)
    x1_ref[0] = x1

    h2 = _rms(x1, n2g_ref[...]) * (1.0 + sc2_ref[0]) + sh2_ref[0]
    h2_ref[0] = h2

    logits = jnp.dot(h2, rw_ref[...], preferred_element_type=F32, precision=lax.Precision.HIGHEST) + rb_ref[...]
    lane = lax.broadcasted_iota(jnp.int32, logits.shape, 1)
    vals = []
    for k in range(TOP_K):
        mx = jnp.max(logits, axis=-1, keepdims=True)
        idx = jnp.min(jnp.where(logits == mx, lane, N_EXPERTS), axis=-1, keepdims=True)
        ti_ref[0, :, k:k + 1] = idx
        vals.append(mx)
        logits = jnp.where(lane == idx, -jnp.inf, logits)
    es = [jnp.exp(v - vals[0]) for v in vals]
    den = es[0] + es[1] + es[2] + es[3]
    for k in range(TOP_K):
        tg_ref[0, :, k:k + 1] = es[k] / den


def _mix(x, a, attn, u, gates, g1, sh2, sc2, p):
    nb, L, _ = x.shape
    tm = min(256, L)
    nh = tm // HALO
    last_h = L // HALO - 1
    tile = lambda w: pl.BlockSpec((1, tm, w), lambda b, i: (b, i, 0))
    prev = pl.BlockSpec((1, HALO, 512), lambda b, i: (b, jnp.maximum(i * nh - 1, 0), 0))
    nxt = pl.BlockSpec((1, HALO, 512), lambda b, i: (b, jnp.minimum((i + 1) * nh, last_h), 0))
    per_b = pl.BlockSpec((1, 1, D_MODEL), lambda b, i: (b, 0, 0))
    full = lambda arr: pl.BlockSpec(arr.shape, lambda b, i: (0,) * arr.ndim)
    weights = [p["w_dw"], p["b_dw"], p["ln_g"], p["ln_b"], p["w_conv_out"], p["w_attn_out"], p["w_pool"],
               p["pool_scale"], p["w_pool_out"], p["w_out"], p["norm2_g"], p["router_w"], p["router_b"]]
    return pl.pallas_call(
        functools.partial(_mix_kernel, seq_len=L),
        grid=(nb, L // tm),
        in_specs=[tile(D_MODEL), tile(D_CONV), prev, nxt, tile(D_Q), tile(D_POOL), prev, nxt,
                  tile(N_BRANCH * D_MODEL), per_b, per_b, per_b] + [full(w) for w in weights],
        out_specs=[tile(D_MODEL), tile(D_MODEL), tile(TOP_K), tile(TOP_K)],
        out_shape=[jax.ShapeDtypeStruct((nb, L, D_MODEL), F32), jax.ShapeDtypeStruct((nb, L, D_MODEL), F32),
                   jax.ShapeDtypeStruct((nb, L, TOP_K), jnp.int32), jax.ShapeDtypeStruct((nb, L, TOP_K), F32)],
        scratch_shapes=[pltpu.VMEM((tm + 2 * HALO, 512), F32)],
        compiler_params=_cparams("parallel", "arbitrary"),
        name="mix",
    )(x, a, a, a, attn, u, u, u, gates, g1, sh2, sc2, *weights)


def _row_copy(src_hbm, row, dst, dst_row, sem):
    return pltpu.make_async_copy(src_hbm.at[pl.ds(row, 1)], dst.at[pl.ds(dst_row, 1)], sem)


def _expert_kernel(be_ref, idx_ref, nidx_ref, h_hbm, wgu_ref, bgu_ref, wd_ref, bd_ref, y_ref, xbuf, sems):
    i = pl.program_id(0)
    n = pl.num_programs(0)
    slot = i % 2
    rows = xbuf.shape[1]

    def issue(ids_ref, s):
        def body(r, carry):
            _row_copy(h_hbm, ids_ref[0, 0, r], xbuf.at[s], r, sems.at[s]).start()
            return carry
        lax.fori_loop(0, rows, body, 0)

    @pl.when(i == 0)
    def _():
        issue(idx_ref, 0)

    @pl.when(i + 1 < n)
    def _():
        issue(nidx_ref, 1 - slot)

    def wait_body(r, carry):
        _row_copy(h_hbm, 0, xbuf.at[slot], r, sems.at[slot]).wait()
        return carry
    lax.fori_loop(0, rows, wait_body, 0)

    xb = xbuf[slot].astype(BF16)
    gu = jnp.dot(xb, wgu_ref[0], preferred_element_type=F32) + bgu_ref[0]
    g = jnp.minimum(gu[:, :D_EXPERT], SWIGLU_LIMIT)
    up = jnp.clip(gu[:, D_EXPERT:], -SWIGLU_LIMIT, SWIGLU_LIMIT)
    act = (up + 1.0) * (g * _sigmoid(SWIGLU_ALPHA * g))
    y_ref[...] = jnp.dot(act.astype(BF16), wd_ref[0], preferred_element_type=F32) + bd_ref[0]


def _experts(h_flat, block_e, row_tok, w_gu_bf, b_gu, w_down_bf, b_down):
    n_blk = block_e.shape[0]
    rows = MOE_ROWS
    ids = row_tok.reshape(n_blk, 1, rows)
    idx_spec = pl.BlockSpec((1, 1, rows), lambda i, be: (i, 0, 0), memory_space=pltpu.SMEM)
    nidx_spec = pl.BlockSpec((1, 1, rows), lambda i, be: (jnp.minimum(i + 1, n_blk - 1), 0, 0),
                             memory_space=pltpu.SMEM)
    grid_spec = pltpu.PrefetchScalarGridSpec(
        num_scalar_prefetch=1,
        grid=(n_blk,),
        in_specs=[idx_spec, nidx_spec, pl.BlockSpec(memory_space=pl.ANY),
                  pl.BlockSpec((1, D_MODEL, 2 * D_EXPERT), lambda i, be: (be[i], 0, 0)),
                  pl.BlockSpec((1, 1, 2 * D_EXPERT), lambda i, be: (be[i], 0, 0)),
                  pl.BlockSpec((1, D_EXPERT, D_MODEL), lambda i, be: (be[i], 0, 0)),
                  pl.BlockSpec((1, 1, D_MODEL), lambda i, be: (be[i], 0, 0))],
        out_specs=pl.BlockSpec((rows, D_MODEL), lambda i, be: (i, 0)),
        scratch_shapes=[pltpu.VMEM((2, rows, D_MODEL), F32), pltpu.SemaphoreType.DMA((2,))],
    )
    return pl.pallas_call(
        _expert_kernel,
        grid_spec=grid_spec,
        out_shape=jax.ShapeDtypeStruct((n_blk * rows, D_MODEL), F32),
        compiler_params=_cparams("arbitrary"),
        name="experts",
    )(block_e, ids, ids, h_flat, w_gu_bf, b_gu.reshape(N_EXPERTS, 1, -1), w_down_bf,
      b_down.reshape(N_EXPERTS, 1, -1))


def _combine_kernel(idx_ref, nidx_ref, y_hbm, x_ref, tg_ref, g2_ref, fg_ref, o_ref, ybuf, sems, *, final):
    i = pl.program_id(0)
    n = pl.num_programs(0)
    slot = i % 2
    tc = x_ref.shape[0]

    def issue(ids_ref, s):
        def body(r, carry):
            for k in range(TOP_K):
                _row_copy(y_hbm, ids_ref[0, 0, r * TOP_K + k], ybuf.at[s, k], r, sems.at[s]).start()
            return carry
        lax.fori_loop(0, tc, body, 0)

    @pl.when(i == 0)
    def _():
        issue(idx_ref, 0)

    @pl.when(i + 1 < n)
    def _():
        issue(nidx_ref, 1 - slot)

    def wait_body(r, carry):
        for k in range(TOP_K):
            _row_copy(y_hbm, 0, ybuf.at[slot, k], r, sems.at[slot]).wait()
        return carry
    lax.fori_loop(0, tc, wait_body, 0)

    f = tg_ref[:, 0:1] * ybuf[slot, 0]
    for k in range(1, TOP_K):
        f = f + tg_ref[:, k:k + 1] * ybuf[slot, k]
    out = x_ref[...] + g2_ref[0] * f
    if final:
        out = _rms(out, fg_ref[...])
    o_ref[...] = out


def _combine(y, x1_flat, dest, tg, g2, final_g, tokens_per_row_group, final):
    T = x1_flat.shape[0]
    tc = COMBINE_ROWS
    n = T // tc
    per_group = tokens_per_row_group // tc
    ids = dest.reshape(n, 1, tc * TOP_K)
    idx_spec = pl.BlockSpec((1, 1, tc * TOP_K), lambda i: (i, 0, 0), memory_space=pltpu.SMEM)
    nidx_spec = pl.BlockSpec((1, 1, tc * TOP_K), lambda i: (jnp.minimum(i + 1, n - 1), 0, 0),
                             memory_space=pltpu.SMEM)
    return pl.pallas_call(
        functools.partial(_combine_kernel, final=final),
        grid=(n,),
        in_specs=[idx_spec, nidx_spec, pl.BlockSpec(memory_space=pl.ANY),
                  pl.BlockSpec((tc, D_MODEL), lambda i: (i, 0)),
                  pl.BlockSpec((tc, TOP_K), lambda i: (i, 0)),
                  pl.BlockSpec((1, 1, D_MODEL), lambda i: (i // per_group, 0, 0)),
                  pl.BlockSpec((1, D_MODEL), lambda i: (0, 0))],
        out_specs=pl.BlockSpec((tc, D_MODEL), lambda i: (i, 0)),
        out_shape=jax.ShapeDtypeStruct((T, D_MODEL), F32),
        scratch_shapes=[pltpu.VMEM((2, TOP_K, tc, D_MODEL), F32), pltpu.SemaphoreType.DMA((2,))],
        compiler_params=_cparams("arbitrary"),
        name="combine_final" if final else "combine",
    )(ids, ids, y, x1_flat, tg, g2, final_g.reshape(1, D_MODEL))


def _route(ti):
    T = ti.shape[0]
    A = T * TOP_K
    rows = MOE_ROWS
    n_blk = -(-A // rows) + N_EXPERTS
    flat_e = ti.reshape(-1)
    order = jnp.argsort(flat_e, stable=True).astype(jnp.int32)
    sorted_e = flat_e[order]
    counts = jnp.zeros((N_EXPERTS,), jnp.int32).at[flat_e].add(1)
    padded = (counts + rows - 1) // rows * rows
    start = jnp.cumsum(counts) - counts
    padded_end = jnp.cumsum(padded)
    padded_start = padded_end - padded
    dest_sorted = padded_start[sorted_e] + (jnp.arange(A, dtype=jnp.int32) - start[sorted_e])
    dest = jnp.zeros((A,), jnp.int32).at[order].set(dest_sorted, unique_indices=True)
    block_e = jnp.minimum(
        jnp.searchsorted(padded_end, jnp.arange(n_blk, dtype=jnp.int32) * rows, side="right"),
        N_EXPERTS - 1).astype(jnp.int32)
    r = jnp.arange(n_blk * rows, dtype=jnp.int32)
    e_of_r = jnp.repeat(block_e, rows)
    j = r - padded_start[e_of_r]
    src = jnp.clip(start[e_of_r] + j, 0, A - 1)
    row_tok = jnp.where((j >= 0) & (j < counts[e_of_r]), order[src] // TOP_K, 0).astype(jnp.int32)
    return block_e, row_tok, dest


def _rope_tables(seq_len):
    rows = seq_len // GRID_W
    row = jnp.repeat(jnp.arange(rows), GRID_W).astype(F32)
    col = jnp.tile(jnp.arange(GRID_W), rows).astype(F32)
    inv = ROPE_THETA ** (-jnp.arange(0, ROPE_AXIS_DIM, 2, dtype=F32) / ROPE_AXIS_DIM)
    ang = jnp.concatenate([row[:, None] * inv, col[:, None] * inv], axis=-1)
    cos, sin = jnp.cos(ang), jnp.sin(ang)
    cos_t = jnp.concatenate([cos, cos, cos, cos], axis=-1)
    sin_t = jnp.concatenate([-sin, sin, -sin, sin], axis=-1)
    return cos_t, sin_t


def kernel(x, c, ctx, c_ctx, w_mod, b_mod, norm1_g, w_in, b_gate, w_dw, b_dw, conv_ln_g, conv_ln_b, w_conv_out,
           attn_sink, w_attn_out, w_pool, pool_scale, w_pool_out, w_out, norm2_g, router_w, router_b, w_gu, b_gu,
           w_down, b_down, final_g):
    B, S, D = x.shape
    C = ctx.shape[1]
    depth = w_mod.shape[0]
    cos_x, sin_x = _rope_tables(S)
    cos_z = jnp.ones((C, 2 * HEAD_DIM), F32)
    sin_z = jnp.zeros((C, 2 * HEAD_DIM), F32)
    c_rows = jnp.zeros((16, D), F32).at[:B].set(c).at[B].set(c_ctx)
    z = ctx
    for l in range(depth):
        last = l == depth - 1
        mod = _modulation(c_rows, w_mod[l], b_mod[l])
        mx = mod[:B].reshape(B, 1, 6, D)
        sh1, sc1, g1, sh2, sc2, g2 = [mx[:, :, j] for j in range(6)]
        mz = jnp.broadcast_to(mod[B].reshape(1, 1, 6, D), (B, 1, 6, D))
        zsh1, zsc1, zg1, zsh2, zsc2, zg2 = [mz[:, :, j] for j in range(6)]
        w_in_bf = w_in[l].astype(BF16)
        p = dict(w_dw=w_dw[l], b_dw=b_dw[l].reshape(1, -1), ln_g=conv_ln_g[l].reshape(1, -1),
                 ln_b=conv_ln_b[l].reshape(1, -1), w_conv_out=w_conv_out[l].astype(BF16),
                 w_attn_out=w_attn_out[l].astype(BF16), w_pool=w_pool[l].astype(BF16),
                 pool_scale=pool_scale[l].reshape(1, -1), w_pool_out=w_pool_out[l].astype(BF16),
                 w_out=w_out[l].astype(BF16), norm2_g=norm2_g[l].reshape(1, -1), router_w=router_w[l],
                 router_b=router_b[l].reshape(1, -1))

        a, q, k, v, u, gates = _inproj(x, sh1, sc1, norm1_g[l], w_in_bf, b_gate[l], cos_x, sin_x)
        za, zq, zk, zv, zu, zgates = _inproj(z, zsh1, zsc1, norm1_g[l], w_in_bf, b_gate[l], cos_z, sin_z)
        attn = _attention(q, k, v, zk, zv, attn_sink[l], local=True)
        x1, h2, ti, tg = _mix(x, a, attn, u, gates, g1, sh2, sc2, p)
        T = B * S
        if last:
            h_all, ti_all = h2.reshape(T, D), ti.reshape(T, TOP_K)
        else:
            zattn = _attention(zq, None, None, zk, zv, attn_sink[l], local=False)
            z1, hz2, zti, ztg = _mix(z, za, zattn, zu, zgates, zg1, zsh2, zsc2, p)
            h_all = jnp.concatenate([h2.reshape(T, D), hz2.reshape(B * C, D)], axis=0)
            ti_all = jnp.concatenate([ti.reshape(T, TOP_K), zti.reshape(B * C, TOP_K)], axis=0)
        block_e, row_tok, dest = _route(ti_all)
        y = _experts(h_all, block_e, row_tok, w_gu[l].astype(BF16), b_gu[l], w_down[l].astype(BF16), b_down[l])
        x = _combine(y, x1.reshape(T, D), dest[:T * TOP_K], tg.reshape(T, TOP_K), g2, final_g, S,
                     final=last).reshape(B, S, D)
        if not last:
            z = _combine(y, z1.reshape(B * C, D), dest[T * TOP_K:], ztg.reshape(B * C, TOP_K), zg2, final_g, C,
                         final=False).reshape(B, C, D)
    return x
```

```python
import functools

import jax
import jax.numpy as jnp
from jax import lax
from jax.experimental import pallas as pl
from jax.experimental.pallas import tpu as pltpu

F32 = jnp.float32
BF16 = jnp.bfloat16

D_MODEL = 1024
GRID_W = 64
N_HEADS = 8
N_KV_HEADS = 2
GROUP = N_HEADS // N_KV_HEADS
HEAD_DIM = 64
ROPE_AXIS_DIM = HEAD_DIM // 2
ROPE_THETA = 10000.0
WINDOW = 128
ATTN_BLOCK = 128
D_CONV = 512
CONV_K = 31
D_POOL = 512
POOL_WINDOWS = (2, 4, 8, 16)
POOL_GROUP = D_POOL // len(POOL_WINDOWS)
N_BRANCH = 3
D_Q = N_HEADS * HEAD_DIM
D_KV = N_KV_HEADS * HEAD_DIM
N_EXPERTS = 32
TOP_K = 4
D_EXPERT = 1024
SWIGLU_LIMIT = 7.0
SWIGLU_ALPHA = 1.702
EPS = 1e-6
MASK_VALUE = -1e30

O_AVAL, O_AGATE, O_Q, O_K, O_V, O_U, O_GATES = 0, 512, 1024, 1536, 1664, 1792, 2304
D_IN = O_GATES + N_BRANCH * D_MODEL

SUBLANES = 8
LANES = 128
HALO = 16
TOKEN_TILE = 256
CONV_ROW_CHUNK = 64
MERGE_COLS = 256
MOE_ROWS = 256
VMEM_LIMIT = 56 * 1024 * 1024


def _cparams(*sem):
    return pltpu.CompilerParams(dimension_semantics=sem, vmem_limit_bytes=VMEM_LIMIT)


def _sigmoid(x):
    return 1.0 / (1.0 + jnp.exp(-x))


def _rms(x, g):
    return x * lax.rsqrt(jnp.mean(x * x, axis=-1, keepdims=True) + EPS) * g


def _combine_rows(x1, g2, tg, ys):
    f = tg[:, 0:1] * ys[0][...]
    for k in range(1, TOP_K):
        f = f + tg[:, k:k + 1] * ys[k][...]
    return x1 + g2 * f


def _mod_kernel(c_ref, w_ref, b_ref, o_ref):
    c = c_ref[...]
    s = c * _sigmoid(c)
    o_ref[...] = jnp.dot(s.astype(BF16), w_ref[...].astype(BF16), preferred_element_type=F32) + b_ref[...]


def _modulation(c_rows, w_mod, b_mod):
    n = w_mod.shape[1]
    tn = 1536
    return pl.pallas_call(
        _mod_kernel,
        grid=(n // tn,),
        in_specs=[pl.BlockSpec((16, D_MODEL), lambda j: (0, 0)),
                  pl.BlockSpec((D_MODEL, tn), lambda j: (0, j)),
                  pl.BlockSpec((1, tn), lambda j: (0, j))],
        out_specs=pl.BlockSpec((16, tn), lambda j: (0, j)),
        out_shape=jax.ShapeDtypeStruct((16, n), F32),
        compiler_params=_cparams("arbitrary"),
        name="modulation",
    )(c_rows, w_mod, b_mod.reshape(1, n))


def _rope(t, cos, sin):
    w = t.shape[1]
    reps = w // cos.shape[1]
    cs = jnp.concatenate([cos] * reps, axis=1) if reps > 1 else cos
    sn = jnp.concatenate([sin] * reps, axis=1) if reps > 1 else sin
    lane = lax.broadcasted_iota(jnp.int32, t.shape, 1)
    first_half = (lane % HEAD_DIM) < (HEAD_DIM // 2)
    partner = jnp.where(first_half, pltpu.roll(t, w - HEAD_DIM // 2, 1), pltpu.roll(t, HEAD_DIM // 2, 1))
    return t * cs + partner * sn


def _inproj_kernel(*refs, combine):
    if combine:
        (x_ref, tg_ref, g2_ref, y0, y1, y2, y3, sh_ref, sc_ref, g_ref, w_ref, bg_ref, cos_ref, sin_ref,
         a_ref, q_ref, k_ref, v_ref, u_ref, gt_ref, xo_ref) = refs
        x = _combine_rows(x_ref[0], g2_ref[0], tg_ref[0], (y0, y1, y2, y3))
        xo_ref[0] = x
    else:
        (x_ref, sh_ref, sc_ref, g_ref, w_ref, bg_ref, cos_ref, sin_ref,
         a_ref, q_ref, k_ref, v_ref, u_ref, gt_ref) = refs
        x = x_ref[0]
    h = _rms(x, g_ref[...]) * (1.0 + sc_ref[0]) + sh_ref[0]
    hb = h.astype(BF16)

    def mm(lo, hi):
        return jnp.dot(hb, w_ref[:, lo:hi], preferred_element_type=F32)

    ag = mm(O_AVAL, O_Q)
    a_ref[0] = ag[:, :D_CONV] * _sigmoid(ag[:, D_CONV:])
    cos = cos_ref[...]
    sin = sin_ref[...]
    q = mm(O_Q, O_K)
    q_ref[0] = (_rope(q, cos, sin) * (HEAD_DIM ** -0.5)).astype(BF16)
    kv = mm(O_K, O_U)
    k_ref[0] = _rope(kv[:, :D_KV], cos, sin).astype(BF16)
    v_ref[0] = kv[:, D_KV:].astype(BF16)
    u_ref[0] = mm(O_U, O_GATES)
    for j in range(N_BRANCH):
        lo = O_GATES + j * D_MODEL
        gl = mm(lo, lo + D_MODEL) + bg_ref[:, j * D_MODEL:(j + 1) * D_MODEL]
        gt_ref[0, :, j * D_MODEL:(j + 1) * D_MODEL] = _sigmoid(gl).astype(BF16)


def _y_specs(tm, L, slot_off):
    per = L // tm
    return [pl.BlockSpec((tm, D_MODEL), functools.partial(
        lambda b, i, k: (slot_off // tm + (b * TOP_K + k) * per + i, 0), k=k)) for k in range(TOP_K)]


def _inproj(x, comb, shift, scale, norm_g, w_in_bf, b_gate, cos_t, sin_t):
    nb, L, _ = x.shape
    tm = min(TOKEN_TILE, L)
    full = lambda shp: pl.BlockSpec(shp, lambda b, i: (0,) * len(shp))
    tile = lambda w: pl.BlockSpec((1, tm, w), lambda b, i: (b, i, 0))
    per_b = pl.BlockSpec((1, 1, D_MODEL), lambda b, i: (b, 0, 0))
    outs = [(D_CONV, F32), (D_Q, BF16), (D_KV, BF16), (D_KV, BF16), (D_POOL, F32), (N_BRANCH * D_MODEL, BF16)]
    in_specs = [tile(D_MODEL)]
    args = [x]
    if comb is not None:
        y, tg, g2, slot_off = comb
        in_specs += [tile(TOP_K), per_b] + _y_specs(tm, L, slot_off)
        args += [tg, g2, y, y, y, y]
        outs = outs + [(D_MODEL, F32)]
    in_specs += [per_b, per_b, full((1, D_MODEL)),
                 pl.BlockSpec((D_MODEL, D_IN), lambda b, i: (0, 0), pipeline_mode=pl.Buffered(1)),
                 full((1, N_BRANCH * D_MODEL)),
                 pl.BlockSpec((tm, 2 * HEAD_DIM), lambda b, i: (i, 0)),
                 pl.BlockSpec((tm, 2 * HEAD_DIM), lambda b, i: (i, 0))]
    args += [shift, scale, norm_g.reshape(1, D_MODEL), w_in_bf, b_gate.reshape(1, -1), cos_t, sin_t]
    return pl.pallas_call(
        functools.partial(_inproj_kernel, combine=comb is not None),
        grid=(nb, L // tm),
        in_specs=in_specs,
        out_specs=[tile(w) for w, _ in outs],
        out_shape=[jax.ShapeDtypeStruct((nb, L, w), dt) for w, dt in outs],
        compiler_params=_cparams("parallel", "arbitrary"),
        name="inproj_combine" if comb is not None else "inproj",
    )(*args)


def _attn_kernel(sink_ref, q_ref, *refs, seq_len, local):
    if local:
        kp_ref, kc_ref, kn_ref, vp_ref, vc_ref, vn_ref, zk_ref, zv_ref, o_ref = refs
    else:
        zk_ref, zv_ref, o_ref = refs
    i = pl.program_id(1)
    q = q_ref[0].astype(F32)
    rows = q.shape[0]
    if local:
        k_all = jnp.concatenate([kp_ref[0], kc_ref[0], kn_ref[0], zk_ref[0]], axis=0).astype(F32)
        v_all = jnp.concatenate([vp_ref[0], vc_ref[0], vn_ref[0], zv_ref[0]], axis=0).astype(F32)
        nk = k_all.shape[0]
        qi = lax.broadcasted_iota(jnp.int32, (rows, nk), 0)
        kj = lax.broadcasted_iota(jnp.int32, (rows, nk), 1)
        kpos = i * ATTN_BLOCK - ATTN_BLOCK + kj
        in_band = jnp.where(jnp.abs(kj - ATTN_BLOCK - qi) <= WINDOW, 1, 0)
        in_range = jnp.where(kpos >= 0, jnp.where(kpos < seq_len, 1, 0), 0)
        is_ctx = jnp.where(kj >= 3 * ATTN_BLOCK, 1, 0)
        valid = (in_band * in_range + is_ctx) > 0
        valid = jnp.concatenate([valid] * GROUP, axis=0)
    else:
        k_all = zk_ref[0].astype(F32)
        v_all = zv_ref[0].astype(F32)
    outs = []
    for g in range(N_KV_HEADS):
        kg = k_all[:, g * HEAD_DIM:(g + 1) * HEAD_DIM].astype(BF16)
        vg = v_all[:, g * HEAD_DIM:(g + 1) * HEAD_DIM].astype(BF16)
        heads = [g * GROUP + h for h in range(GROUP)]
        qg = jnp.concatenate([q[:, hh * HEAD_DIM:(hh + 1) * HEAD_DIM] for hh in heads], axis=0).astype(BF16)
        s = lax.dot_general(qg, kg, (((1,), (1,)), ((), ())), preferred_element_type=F32)
        if local:
            s = jnp.where(valid, s, MASK_VALUE)
        sink = jnp.concatenate([jnp.full((rows, 1), sink_ref[hh], F32) for hh in heads], axis=0)
        m = jnp.maximum(jnp.max(s, axis=-1, keepdims=True), sink)
        p = jnp.exp(s - m)
        den = jnp.sum(p, axis=-1, keepdims=True) + jnp.exp(sink - m)
        o = jnp.dot(p.astype(BF16), vg, preferred_element_type=F32) / den
        outs += [o[h * rows:(h + 1) * rows] for h in range(GROUP)]
    o_ref[0] = jnp.concatenate(outs, axis=1).astype(BF16)


def _attention(q, k, v, zk, zv, sink, local):
    nb, L, _ = q.shape
    C = zk.shape[1]
    tq = ATTN_BLOCK if local else L
    nblk = L // tq
    qspec = pl.BlockSpec((1, tq, D_Q), lambda b, i: (b, i, 0))
    zspec = pl.BlockSpec((1, C, D_KV), lambda b, i: (b, 0, 0))
    if local:
        prev = pl.BlockSpec((1, tq, D_KV), lambda b, i: (b, jnp.maximum(i - 1, 0), 0))
        cur = pl.BlockSpec((1, tq, D_KV), lambda b, i: (b, i, 0))
        nxt = pl.BlockSpec((1, tq, D_KV), lambda b, i: (b, jnp.minimum(i + 1, nblk - 1), 0))
        in_specs = [qspec, prev, cur, nxt, prev, cur, nxt, zspec, zspec]
        args = (q, k, k, k, v, v, v, zk, zv)
    else:
        in_specs = [qspec, zspec, zspec]
        args = (q, zk, zv)
    return pl.pallas_call(
        functools.partial(_attn_kernel, seq_len=L, local=local),
        grid=(nb, nblk),
        in_specs=[pl.BlockSpec(memory_space=pltpu.SMEM)] + in_specs,
        out_specs=qspec,
        out_shape=jax.ShapeDtypeStruct((nb, L, D_Q), BF16),
        compiler_params=_cparams("parallel", "arbitrary"),
        name="attention_local" if local else "attention_ctx",
    )(sink, *args)


def _mix_kernel(*refs, seq_len, aliased):
    if aliased:
        refs = refs[1:]
    (x_ref, a_ref, ap_ref, an_ref, at_ref, u_ref, up_ref, un_ref, gt_ref, g1_ref, sh2_ref, sc2_ref,
     wdw_ref, bdw_ref, lng_ref, lnb_ref, wco_ref, wao_ref, wpool_ref, psc_ref, wpo_ref, wout_ref,
     n2g_ref, rwt_ref, rb_ref, x1_ref, h2_ref, ti_ref, tg_ref, ext_ref, conv_ref, mrg_ref) = refs
    i = pl.program_id(1)
    nt = pl.num_programs(1)
    tm = x_ref.shape[1]
    has_prev = (i > 0).astype(F32)
    has_next = (i < nt - 1).astype(F32)

    def fill_ext(c_ref, p_ref, n_ref):
        ext_ref[0:HALO, :] = p_ref[0] * has_prev
        ext_ref[HALO:HALO + tm, :] = c_ref[0]
        ext_ref[HALO + tm:HALO + tm + HALO, :] = n_ref[0] * has_next

    fill_ext(a_ref, ap_ref, an_ref)
    base = HALO - CONV_K // 2
    rc = CONV_ROW_CHUNK
    for cb in range(D_CONV // LANES):
        sl = slice(cb * LANES, (cb + 1) * LANES)
        for r0 in range(0, tm, rc):
            acc = jnp.zeros((rc, LANES), F32) + bdw_ref[:, sl]
            for ph in range(SUBLANES):
                part = None
                for m in range((base + CONV_K - 1) // SUBLANES + 1):
                    j = m * SUBLANES + ph - base
                    if 0 <= j < CONV_K:
                        term = ext_ref[pl.ds(r0 + m * SUBLANES, rc + SUBLANES), sl] * wdw_ref[j:j + 1, sl]
                        part = term if part is None else part + term
                acc = acc + part[ph:ph + rc]
            conv_ref[r0:r0 + rc, sl] = acc
    acc = conv_ref[...]
    mu = jnp.mean(acc, axis=-1, keepdims=True)
    cen = acc - mu
    var = jnp.mean(cen * cen, axis=-1, keepdims=True)
    ln = cen * lax.rsqrt(var + EPS) * lng_ref[...] + lnb_ref[...]
    act = (ln * _sigmoid(ln)).astype(BF16)

    fill_ext(u_ref, up_ref, un_ref)
    t = i * tm + lax.broadcasted_iota(jnp.int32, (tm, 1), 0)
    u = u_ref[0]
    mixed = []
    for gi, w in enumerate(POOL_WINDOWS):
        sl = slice(gi * POOL_GROUP, (gi + 1) * POOL_GROUP)
        run = ext_ref[:, sl]
        span = 1
        while span < min(w, SUBLANES):
            run = run[:run.shape[0] - span] + run[span:]
            span *= 2
        lo = HALO - w // 2
        tot = run[lo:lo + tm]
        for extra in range(span, w, span):
            tot = tot + run[lo + extra:lo + extra + tm]
        cnt = (jnp.minimum(t + (w - w // 2), seq_len) - jnp.maximum(t - w // 2, 0)).astype(F32)
        pooled = tot / cnt - u[:, sl]
        mixed.append(jnp.dot(pooled.astype(BF16), wpool_ref[gi], preferred_element_type=F32))
    mixed = (jnp.concatenate(mixed, axis=1) * psc_ref[...]).astype(BF16)
    attn = at_ref[0]

    for c0 in range(0, D_MODEL, MERGE_COLS):
        cs = slice(c0, c0 + MERGE_COLS)
        branches = ((act, wco_ref), (attn, wao_ref), (mixed, wpo_ref))
        m = None
        for j, (lhs, w_ref) in enumerate(branches):
            gate = gt_ref[0, :, j * D_MODEL + c0:j * D_MODEL + c0 + MERGE_COLS].astype(F32)
            term = gate * jnp.dot(lhs, w_ref[:, cs], preferred_element_type=F32)
            m = term if m is None else m + term
        mrg_ref[:, cs] = m.astype(BF16)
    x1 = x_ref[0] + g1_ref[0] * jnp.dot(mrg_ref[...], wout_ref[...], preferred_element_type=F32)
    x1_ref[0] = x1

    h2 = _rms(x1, n2g_ref[...]) * (1.0 + sc2_ref[0]) + sh2_ref[0]
    h2_ref[...] = h2

    logits = lax.dot_general(rwt_ref[...], h2, (((1,), (1,)), ((), ())), preferred_element_type=F32,
                             precision=lax.Precision.HIGHEST) + rb_ref[...]
    eid = lax.broadcasted_iota(jnp.int32, logits.shape, 0)
    vals = []
    for k in range(TOP_K):
        mx = jnp.max(logits, axis=0, keepdims=True)
        idx = jnp.min(jnp.where(logits == mx, eid, N_EXPERTS), axis=0, keepdims=True)
        ti_ref[0, k:k + 1, :] = idx
        vals.append(mx)
        logits = jnp.where(eid == idx, -jnp.inf, logits)
    es = [jnp.exp(v - vals[0]) for v in vals]
    den = es[0] + es[1] + es[2] + es[3]
    for k in range(TOP_K):
        tg_ref[0, k:k + 1, :] = es[k] / den


def _mix(x, a, attn, u, gates, g1, sh2, sc2, p, h_all, row_off, total_rows):
    nb, L, _ = x.shape
    tm = min(TOKEN_TILE, L)
    nh = tm // HALO
    last_h = L // HALO - 1
    per = L // tm
    tile = lambda w: pl.BlockSpec((1, tm, w), lambda b, i: (b, i, 0))
    prev = pl.BlockSpec((1, HALO, 512), lambda b, i: (b, jnp.maximum(i * nh - 1, 0), 0))
    nxt = pl.BlockSpec((1, HALO, 512), lambda b, i: (b, jnp.minimum((i + 1) * nh, last_h), 0))
    per_b = pl.BlockSpec((1, 1, D_MODEL), lambda b, i: (b, 0, 0))
    full = lambda arr: pl.BlockSpec(arr.shape, lambda b, i: (0,) * arr.ndim)
    route = pl.BlockSpec((1, TOP_K, tm), lambda b, i: (b, 0, i))
    weights = [p["w_dw"], p["b_dw"], p["ln_g"], p["ln_b"], p["w_conv_out"], p["w_attn_out"], p["w_pool"],
               p["pool_scale"], p["w_pool_out"], p["w_out"], p["norm2_g"], p["router_wt"], p["router_b"]]
    in_specs = [tile(D_MODEL), tile(D_CONV), prev, nxt, tile(D_Q), tile(D_POOL), prev, nxt,
                tile(N_BRANCH * D_MODEL), per_b, per_b, per_b] + [full(w) for w in weights]
    args = [x, a, a, a, attn, u, u, u, gates, g1, sh2, sc2, *weights]
    aliases = {}
    if h_all is not None:
        in_specs = [pl.BlockSpec(memory_space=pl.ANY)] + in_specs
        args = [h_all] + args
        aliases = {0: 1}
    return pl.pallas_call(
        functools.partial(_mix_kernel, seq_len=L, aliased=h_all is not None),
        grid=(nb, per),
        in_specs=in_specs,
        out_specs=[tile(D_MODEL), pl.BlockSpec((tm, D_MODEL), lambda b, i: (row_off // tm + b * per + i, 0)),
                   route, route],
        out_shape=[jax.ShapeDtypeStruct((nb, L, D_MODEL), F32), jax.ShapeDtypeStruct((total_rows, D_MODEL), F32),
                   jax.ShapeDtypeStruct((nb, TOP_K, L), jnp.int32), jax.ShapeDtypeStruct((nb, TOP_K, L), F32)],
        scratch_shapes=[pltpu.VMEM((tm + 2 * HALO, 512), F32), pltpu.VMEM((tm, D_CONV), F32),
                        pltpu.VMEM((tm, D_MODEL), BF16)],
        input_output_aliases=aliases,
        compiler_params=_cparams("parallel", "arbitrary"),
        name="mix",
    )(*args)


def _row_copy(src, src_row, dst, dst_row, sem):
    return pltpu.make_async_copy(src.at[pl.ds(src_row, 1)], dst.at[pl.ds(dst_row, 1)], sem)


def _expert_kernel(be_ref, tok_ref, ntok_ref, pslot_ref, slot_ref, h_hbm, wgu_ref, bgu_ref, wd_ref, bd_ref, y_hbm,
                   xbuf, ybuf, xs, wgu_bf, wd_bf, gsem, ssem):
    i = pl.program_id(0)
    n = pl.num_programs(0)
    cur = i % 2
    rows = xbuf.shape[1]

    def gather(ids_ref, s, wait):
        for r in range(rows):
            cp = _row_copy(h_hbm, 0 if wait else ids_ref[0, 0, r], xbuf.at[s], r, gsem.at[s])
            cp.wait() if wait else cp.start()

    def scatter(ids_ref, s, wait):
        for r in range(rows):
            cp = _row_copy(ybuf.at[s], r, y_hbm, 0 if wait else ids_ref[0, 0, r], ssem.at[s])
            cp.wait() if wait else cp.start()

    @pl.when(i == 0)
    def _():
        gather(tok_ref, 0, False)
        ybuf[1] = jnp.zeros(ybuf.shape[1:], F32)

    @pl.when(i > 0)
    def _():
        scatter(None, cur, True)

    @pl.when(jnp.logical_or(i == 0, be_ref[i] != be_ref[jnp.maximum(i - 1, 0)]))
    def _():
        wgu_bf[...] = wgu_ref[0].astype(BF16)
        wd_bf[...] = wd_ref[0].astype(BF16)

    gather(None, cur, True)
    xs[...] = xbuf[cur].astype(BF16)
    gather(ntok_ref, 1 - cur, False)
    scatter(pslot_ref, 1 - cur, False)
    gu = jnp.dot(xs[...], wgu_bf[...], preferred_element_type=F32) + bgu_ref[0]
    g = jnp.minimum(gu[:, :D_EXPERT], SWIGLU_LIMIT)
    up = jnp.clip(gu[:, D_EXPERT:], -SWIGLU_LIMIT, SWIGLU_LIMIT)
    act = (up + 1.0) * (g * _sigmoid(SWIGLU_ALPHA * g))
    ybuf[cur] = jnp.dot(act.astype(BF16), wd_bf[...], preferred_element_type=F32) + bd_ref[0]

    @pl.when(i == n - 1)
    def _():
        scatter(None, 1 - cur, True)
        scatter(slot_ref, cur, False)
        scatter(None, cur, True)
        gather(None, 1 - cur, True)


def _experts(h_flat, block_e, row_tok, row_slot, w_gu, b_gu, w_down, b_down):
    n_blk = block_e.shape[0]
    rows = MOE_ROWS
    P = n_blk * rows
    smem_blk = lambda f: pl.BlockSpec((1, 1, rows), f, memory_space=pltpu.SMEM)
    grid_spec = pltpu.PrefetchScalarGridSpec(
        num_scalar_prefetch=1,
        grid=(n_blk,),
        in_specs=[smem_blk(lambda i, be: (i, 0, 0)),
                  smem_blk(lambda i, be: (jnp.minimum(i + 1, n_blk - 1), 0, 0)),
                  smem_blk(lambda i, be: (i, 0, 0)),
                  smem_blk(lambda i, be: (i + 1, 0, 0)),
                  pl.BlockSpec(memory_space=pl.ANY),
                  pl.BlockSpec((1, D_MODEL, 2 * D_EXPERT), lambda i, be: (be[i], 0, 0)),
                  pl.BlockSpec((1, 1, 2 * D_EXPERT), lambda i, be: (be[i], 0, 0)),
                  pl.BlockSpec((1, D_EXPERT, D_MODEL), lambda i, be: (be[i], 0, 0)),
                  pl.BlockSpec((1, 1, D_MODEL), lambda i, be: (be[i], 0, 0))],
        out_specs=pl.BlockSpec(memory_space=pl.ANY),
        scratch_shapes=[pltpu.VMEM((2, rows, D_MODEL), F32), pltpu.VMEM((2, rows, D_MODEL), F32),
                        pltpu.VMEM((rows, D_MODEL), BF16), pltpu.VMEM((D_MODEL, 2 * D_EXPERT), BF16), pltpu.VMEM((D_EXPERT, D_MODEL), BF16),
                        pltpu.SemaphoreType.DMA((2,)), pltpu.SemaphoreType.DMA((2,))],
    )
    tok = row_tok.reshape(n_blk, 1, rows)
    slots = jnp.concatenate([P + jnp.arange(rows, dtype=jnp.int32), row_slot]).reshape(n_blk + 1, 1, rows)
    return pl.pallas_call(
        _expert_kernel,
        grid_spec=grid_spec,
        out_shape=jax.ShapeDtypeStruct((P + rows, D_MODEL), F32),
        compiler_params=_cparams("arbitrary"),
        name="experts",
    )(block_e, tok, tok, slots, slots, h_flat, w_gu, b_gu.reshape(N_EXPERTS, 1, -1),
      w_down, b_down.reshape(N_EXPERTS, 1, -1))


def _route(ti_flat, tok_of_slot):
    A = ti_flat.shape[0]
    rows = MOE_ROWS
    n_blk = -(-A // rows) + N_EXPERTS
    P = n_blk * rows
    order = jnp.argsort(ti_flat, stable=True).astype(jnp.int32)
    experts = jnp.arange(N_EXPERTS, dtype=jnp.int32)
    counts = jnp.sum((ti_flat[:, None] == experts[None, :]).astype(jnp.int32), axis=0)
    padded = (counts + rows - 1) // rows * rows
    start = jnp.cumsum(counts) - counts
    padded_end = jnp.cumsum(padded)
    padded_start = padded_end - padded
    blk_row0 = jnp.arange(n_blk, dtype=jnp.int32) * rows
    block_e = jnp.minimum(jnp.sum((padded_end[None, :] <= blk_row0[:, None]).astype(jnp.int32), axis=1),
                          N_EXPERTS - 1).astype(jnp.int32)
    r = jnp.arange(P, dtype=jnp.int32)
    e_of_r = jnp.repeat(block_e, rows)
    j = r - padded_start[e_of_r]
    valid = (j >= 0) & (j < counts[e_of_r])
    slot = order[jnp.clip(start[e_of_r] + j, 0, A - 1)]
    pad_rank = jnp.cumsum(jnp.where(valid, 0, 1).astype(jnp.int32)) - 1
    row_slot = jnp.where(valid, slot, A + pad_rank).astype(jnp.int32)
    row_tok = jnp.where(valid, tok_of_slot[slot], 0).astype(jnp.int32)
    return block_e, row_tok, row_slot


def _slot_tokens(nb, L, row_off):
    b = jnp.arange(nb, dtype=jnp.int32)[:, None, None]
    t = jnp.arange(L, dtype=jnp.int32)[None, None, :]
    return jnp.broadcast_to(row_off + b * L + t, (nb, TOP_K, L)).reshape(-1)


def _final_kernel(x_ref, tg_ref, g2_ref, y0, y1, y2, y3, fg_ref, o_ref):
    o_ref[0] = _rms(_combine_rows(x_ref[0], g2_ref[0], tg_ref[0], (y0, y1, y2, y3)), fg_ref[...])


def _final(x1, y, tg, g2, final_g):
    nb, L, _ = x1.shape
    tm = min(TOKEN_TILE, L)
    tile = lambda w: pl.BlockSpec((1, tm, w), lambda b, i: (b, i, 0))
    return pl.pallas_call(
        _final_kernel,
        grid=(nb, L // tm),
        in_specs=[tile(D_MODEL), tile(TOP_K), pl.BlockSpec((1, 1, D_MODEL), lambda b, i: (b, 0, 0))]
        + _y_specs(tm, L, 0) + [pl.BlockSpec((1, D_MODEL), lambda b, i: (0, 0))],
        out_specs=tile(D_MODEL),
        out_shape=jax.ShapeDtypeStruct((nb, L, D_MODEL), F32),
        compiler_params=_cparams("parallel", "arbitrary"),
        name="final_combine_norm",
    )(x1, tg, g2, y, y, y, y, final_g.reshape(1, D_MODEL))


def _rope_tables(seq_len):
    rows = seq_len // GRID_W
    row = jnp.repeat(jnp.arange(rows), GRID_W).astype(F32)
    col = jnp.tile(jnp.arange(GRID_W), rows).astype(F32)
    inv = ROPE_THETA ** (-jnp.arange(0, ROPE_AXIS_DIM, 2, dtype=F32) / ROPE_AXIS_DIM)
    ang = jnp.concatenate([row[:, None] * inv, col[:, None] * inv], axis=-1)
    cos, sin = jnp.cos(ang), jnp.sin(ang)
    cos_t = jnp.concatenate([cos, cos, cos, cos], axis=-1)
    sin_t = jnp.concatenate([-sin, sin, -sin, sin], axis=-1)
    return cos_t, sin_t


def kernel(x, c, ctx, c_ctx, w_mod, b_mod, norm1_g, w_in, b_gate, w_dw, b_dw, conv_ln_g, conv_ln_b, w_conv_out,
           attn_sink, w_attn_out, w_pool, pool_scale, w_pool_out, w_out, norm2_g, router_w, router_b, w_gu, b_gu,
           w_down, b_down, final_g):
    B, S, D = x.shape
    C = ctx.shape[1]
    T, TZ = B * S, B * C
    depth = w_mod.shape[0]
    cos_x, sin_x = _rope_tables(S)
    cos_z = jnp.ones((C, 2 * HEAD_DIM), F32)
    sin_z = jnp.zeros((C, 2 * HEAD_DIM), F32)
    c_rows = jnp.zeros((16, D), F32).at[:B].set(c).at[B].set(c_ctx)
    z = ctx
    comb_x = comb_z = None
    for l in range(depth):
        last = l == depth - 1
        mod = _modulation(c_rows, w_mod[l], b_mod[l])
        mx = mod[:B].reshape(B, 1, 6, D)
        sh1, sc1, g1, sh2, sc2, g2 = [mx[:, :, j] for j in range(6)]
        mz = jnp.broadcast_to(mod[B].reshape(1, 1, 6, D), (B, 1, 6, D))
        zsh1, zsc1, zg1, zsh2, zsc2, zg2 = [mz[:, :, j] for j in range(6)]
        w_in_bf = w_in[l].astype(BF16)
        p = dict(w_dw=w_dw[l], b_dw=b_dw[l].reshape(1, -1), ln_g=conv_ln_g[l].reshape(1, -1),
                 ln_b=conv_ln_b[l].reshape(1, -1), w_conv_out=w_conv_out[l].astype(BF16),
                 w_attn_out=w_attn_out[l].astype(BF16), w_pool=w_pool[l].astype(BF16),
                 pool_scale=pool_scale[l].reshape(1, -1), w_pool_out=w_pool_out[l].astype(BF16),
                 w_out=w_out[l].astype(BF16), norm2_g=norm2_g[l].reshape(1, -1), router_wt=router_w[l].T,
                 router_b=router_b[l].reshape(-1, 1))

        res = _inproj(x, comb_x, sh1, sc1, norm1_g[l], w_in_bf, b_gate[l], cos_x, sin_x)
        a, q, k, v, u, gates = res[:6]
        x = res[6] if comb_x is not None else x
        zres = _inproj(z, comb_z, zsh1, zsc1, norm1_g[l], w_in_bf, b_gate[l], cos_z, sin_z)
        za, zq, zk, zv, zu, zgates = zres[:6]
        z = zres[6] if comb_z is not None else z

        attn = _attention(q, k, v, zk, zv, attn_sink[l], local=True)
        rows_all = T if last else T + TZ
        x1, h_all, ti, tg = _mix(x, a, attn, u, gates, g1, sh2, sc2, p, None, 0, rows_all)
        ti_flat, tok_of_slot = ti.reshape(-1), _slot_tokens(B, S, 0)
        if not last:
            zattn = _attention(zq, None, None, zk, zv, attn_sink[l], local=False)
            z1, h_all, zti, ztg = _mix(z, za, zattn, zu, zgates, zg1, zsh2, zsc2, p, h_all, T, rows_all)
            ti_flat = jnp.concatenate([ti_flat, zti.reshape(-1)])
            tok_of_slot = jnp.concatenate([tok_of_slot, _slot_tokens(B, C, T)])
        block_e, row_tok, row_slot = _route(ti_flat, tok_of_slot)
        y = _experts(h_all, block_e, row_tok, row_slot, w_gu[l], b_gu[l], w_down[l], b_down[l])
        comb_x = (y, jnp.swapaxes(tg, 1, 2), g2, 0)
        x = x1
        if not last:
            comb_z = (y, jnp.swapaxes(ztg, 1, 2), zg2, T * TOP_K)
            z = z1
    y, tg_rows, g2, _ = comb_x
    return _final(x, y, tg_rows, g2, final_g)
```

```python
import functools

import jax
import jax.numpy as jnp
from jax import lax
from jax.experimental import pallas as pl
from jax.experimental.pallas import tpu as pltpu

F32 = jnp.float32
BF16 = jnp.bfloat16

D_MODEL = 1024
GRID_W = 64
N_HEADS = 8
N_KV_HEADS = 2
GROUP = N_HEADS // N_KV_HEADS
HEAD_DIM = 64
ROPE_AXIS_DIM = HEAD_DIM // 2
ROPE_THETA = 10000.0
WINDOW = 128
ATTN_BLOCK = 128
D_CONV = 512
CONV_K = 31
D_POOL = 512
POOL_WINDOWS = (2, 4, 8, 16)
POOL_GROUP = D_POOL // len(POOL_WINDOWS)
N_BRANCH = 3
D_Q = N_HEADS * HEAD_DIM
D_KV = N_KV_HEADS * HEAD_DIM
N_EXPERTS = 32
TOP_K = 4
D_EXPERT = 1024
SWIGLU_LIMIT = 7.0
SWIGLU_ALPHA = 1.702
EPS = 1e-6
MASK_VALUE = -1e30

O_AVAL, O_AGATE, O_Q, O_K, O_V, O_U, O_GATES = 0, 512, 1024, 1536, 1664, 1792, 2304
D_IN = O_GATES + N_BRANCH * D_MODEL

SUBLANES = 8
LANES = 128
ROW_CHUNKS = D_MODEL // LANES
HALO = 16
TOKEN_TILE = 256
CONV_ROW_CHUNK = 64
MERGE_COLS = 256
MOE_ROWS = 256
VMEM_LIMIT = 56 * 1024 * 1024


def _cparams(*sem):
    return pltpu.CompilerParams(dimension_semantics=sem, vmem_limit_bytes=VMEM_LIMIT)


def _sigmoid(x):
    return 1.0 / (1.0 + jnp.exp(-x))


def _rms(x, g):
    return x * lax.rsqrt(jnp.mean(x * x, axis=-1, keepdims=True) + EPS) * g


def _load_token_rows(ref, lead, rows):
    return jnp.concatenate([ref[(*lead, pl.ds(c, rows, stride=ROW_CHUNKS), slice(None))]
                            for c in range(ROW_CHUNKS)], axis=1)


def _store_token_rows(ref, lead, val):
    rows = val.shape[0]
    for c in range(ROW_CHUNKS):
        ref[(*lead, pl.ds(c, rows, stride=ROW_CHUNKS), slice(None))] = val[:, c * LANES:(c + 1) * LANES]


def _combine_rows(x1, g2, tg, ys):
    tm = x1.shape[0]
    f = tg[:, 0:1] * _load_token_rows(ys[0], (), tm)
    for k in range(1, TOP_K):
        f = f + tg[:, k:k + 1] * _load_token_rows(ys[k], (), tm)
    return x1 + g2 * f


def _mod_kernel(c_ref, w_ref, b_ref, o_ref):
    c = c_ref[...]
    s = c * _sigmoid(c)
    o_ref[...] = jnp.dot(s.astype(BF16), w_ref[...].astype(BF16), preferred_element_type=F32) + b_ref[...]


def _modulation(c_rows, w_mod, b_mod):
    n = w_mod.shape[1]
    tn = 1536
    return pl.pallas_call(
        _mod_kernel,
        grid=(n // tn,),
        in_specs=[pl.BlockSpec((16, D_MODEL), lambda j: (0, 0)),
                  pl.BlockSpec((D_MODEL, tn), lambda j: (0, j)),
                  pl.BlockSpec((1, tn), lambda j: (0, j))],
        out_specs=pl.BlockSpec((16, tn), lambda j: (0, j)),
        out_shape=jax.ShapeDtypeStruct((16, n), F32),
        compiler_params=_cparams("arbitrary"),
        name="modulation",
    )(c_rows, w_mod, b_mod.reshape(1, n))


def _rope(t, cos, sin):
    w = t.shape[1]
    reps = w // cos.shape[1]
    cs = jnp.concatenate([cos] * reps, axis=1) if reps > 1 else cos
    sn = jnp.concatenate([sin] * reps, axis=1) if reps > 1 else sin
    lane = lax.broadcasted_iota(jnp.int32, t.shape, 1)
    first_half = (lane % HEAD_DIM) < (HEAD_DIM // 2)
    partner = jnp.where(first_half, pltpu.roll(t, w - HEAD_DIM // 2, 1), pltpu.roll(t, HEAD_DIM // 2, 1))
    return t * cs + partner * sn


def _inproj_kernel(*refs, combine):
    if combine:
        (x_ref, tg_ref, g2_ref, y0, y1, y2, y3, sh_ref, sc_ref, g_ref, w_ref, bg_ref, cos_ref, sin_ref,
         a_ref, q_ref, k_ref, v_ref, u_ref, gt_ref, xo_ref) = refs
        x = _combine_rows(x_ref[0], g2_ref[0], tg_ref[0], (y0, y1, y2, y3))
        xo_ref[0] = x
    else:
        (x_ref, sh_ref, sc_ref, g_ref, w_ref, bg_ref, cos_ref, sin_ref,
         a_ref, q_ref, k_ref, v_ref, u_ref, gt_ref) = refs
        x = x_ref[0]
    h = _rms(x, g_ref[...]) * (1.0 + sc_ref[0]) + sh_ref[0]
    hb = h.astype(BF16)

    def mm(lo, hi):
        return jnp.dot(hb, w_ref[:, lo:hi], preferred_element_type=F32)

    ag = mm(O_AVAL, O_Q)
    a_ref[0] = ag[:, :D_CONV] * _sigmoid(ag[:, D_CONV:])
    cos = cos_ref[...]
    sin = sin_ref[...]
    q = mm(O_Q, O_K)
    q_ref[0] = (_rope(q, cos, sin) * (HEAD_DIM ** -0.5)).astype(BF16)
    kv = mm(O_K, O_U)
    k_ref[0] = _rope(kv[:, :D_KV], cos, sin).astype(BF16)
    v_ref[0] = kv[:, D_KV:].astype(BF16)
    u_ref[0] = mm(O_U, O_GATES)
    for j in range(N_BRANCH):
        lo = O_GATES + j * D_MODEL
        gl = mm(lo, lo + D_MODEL) + bg_ref[:, j * D_MODEL:(j + 1) * D_MODEL]
        gt_ref[0, :, j * D_MODEL:(j + 1) * D_MODEL] = _sigmoid(gl).astype(BF16)


def _y_specs(tm, L, slot_off):
    per = L // tm
    return [pl.BlockSpec((tm * ROW_CHUNKS, LANES), functools.partial(
        lambda b, i, k: (slot_off // tm + (b * TOP_K + k) * per + i, 0), k=k)) for k in range(TOP_K)]


def _inproj(x, comb, shift, scale, norm_g, w_in_bf, b_gate, cos_t, sin_t):
    nb, L, _ = x.shape
    tm = min(TOKEN_TILE, L)
    full = lambda shp: pl.BlockSpec(shp, lambda b, i: (0,) * len(shp))
    tile = lambda w: pl.BlockSpec((1, tm, w), lambda b, i: (b, i, 0))
    per_b = pl.BlockSpec((1, 1, D_MODEL), lambda b, i: (b, 0, 0))
    outs = [(D_CONV, F32), (D_Q, BF16), (D_KV, BF16), (D_KV, BF16), (D_POOL, F32), (N_BRANCH * D_MODEL, BF16)]
    in_specs = [tile(D_MODEL)]
    args = [x]
    if comb is not None:
        y, tg, g2, slot_off = comb
        in_specs += [tile(TOP_K), per_b] + _y_specs(tm, L, slot_off)
        args += [tg, g2, y, y, y, y]
        outs = outs + [(D_MODEL, F32)]
    in_specs += [per_b, per_b, full((1, D_MODEL)),
                 pl.BlockSpec((D_MODEL, D_IN), lambda b, i: (0, 0), pipeline_mode=pl.Buffered(1)),
                 full((1, N_BRANCH * D_MODEL)),
                 pl.BlockSpec((tm, 2 * HEAD_DIM), lambda b, i: (i, 0)),
                 pl.BlockSpec((tm, 2 * HEAD_DIM), lambda b, i: (i, 0))]
    args += [shift, scale, norm_g.reshape(1, D_MODEL), w_in_bf, b_gate.reshape(1, -1), cos_t, sin_t]
    return pl.pallas_call(
        functools.partial(_inproj_kernel, combine=comb is not None),
        grid=(nb, L // tm),
        in_specs=in_specs,
        out_specs=[tile(w) for w, _ in outs],
        out_shape=[jax.ShapeDtypeStruct((nb, L, w), dt) for w, dt in outs],
        compiler_params=_cparams("parallel", "arbitrary"),
        name="inproj_combine" if comb is not None else "inproj",
    )(*args)


def _attn_kernel(sink_ref, q_ref, *refs, seq_len, local):
    if local:
        kp_ref, kc_ref, kn_ref, vp_ref, vc_ref, vn_ref, zk_ref, zv_ref, o_ref = refs
    else:
        zk_ref, zv_ref, o_ref = refs
    i = pl.program_id(1)
    q = q_ref[0].astype(F32)
    rows = q.shape[0]
    if local:
        k_all = jnp.concatenate([kp_ref[0], kc_ref[0], kn_ref[0], zk_ref[0]], axis=0).astype(F32)
        v_all = jnp.concatenate([vp_ref[0], vc_ref[0], vn_ref[0], zv_ref[0]], axis=0).astype(F32)
        nk = k_all.shape[0]
        qi = lax.broadcasted_iota(jnp.int32, (rows, nk), 0)
        kj = lax.broadcasted_iota(jnp.int32, (rows, nk), 1)
        kpos = i * ATTN_BLOCK - ATTN_BLOCK + kj
        in_band = jnp.where(jnp.abs(kj - ATTN_BLOCK - qi) <= WINDOW, 1, 0)
        in_range = jnp.where(kpos >= 0, jnp.where(kpos < seq_len, 1, 0), 0)
        is_ctx = jnp.where(kj >= 3 * ATTN_BLOCK, 1, 0)
        valid = (in_band * in_range + is_ctx) > 0
        valid = jnp.concatenate([valid] * GROUP, axis=0)
    else:
        k_all = zk_ref[0].astype(F32)
        v_all = zv_ref[0].astype(F32)
    outs = []
    for g in range(N_KV_HEADS):
        kg = k_all[:, g * HEAD_DIM:(g + 1) * HEAD_DIM].astype(BF16)
        vg = v_all[:, g * HEAD_DIM:(g + 1) * HEAD_DIM].astype(BF16)
        heads = [g * GROUP + h for h in range(GROUP)]
        qg = jnp.concatenate([q[:, hh * HEAD_DIM:(hh + 1) * HEAD_DIM] for hh in heads], axis=0).astype(BF16)
        s = lax.dot_general(qg, kg, (((1,), (1,)), ((), ())), preferred_element_type=F32)
        if local:
            s = jnp.where(valid, s, MASK_VALUE)
        sink = jnp.concatenate([jnp.full((rows, 1), sink_ref[hh], F32) for hh in heads], axis=0)
        m = jnp.maximum(jnp.max(s, axis=-1, keepdims=True), sink)
        p = jnp.exp(s - m)
        den = jnp.sum(p, axis=-1, keepdims=True) + jnp.exp(sink - m)
        o = jnp.dot(p.astype(BF16), vg, preferred_element_type=F32) / den
        outs += [o[h * rows:(h + 1) * rows] for h in range(GROUP)]
    o_ref[0] = jnp.concatenate(outs, axis=1).astype(BF16)


def _attention(q, k, v, zk, zv, sink, local):
    nb, L, _ = q.shape
    C = zk.shape[1]
    tq = ATTN_BLOCK if local else L
    nblk = L // tq
    qspec = pl.BlockSpec((1, tq, D_Q), lambda b, i: (b, i, 0))
    zspec = pl.BlockSpec((1, C, D_KV), lambda b, i: (b, 0, 0))
    if local:
        prev = pl.BlockSpec((1, tq, D_KV), lambda b, i: (b, jnp.maximum(i - 1, 0), 0))
        cur = pl.BlockSpec((1, tq, D_KV), lambda b, i: (b, i, 0))
        nxt = pl.BlockSpec((1, tq, D_KV), lambda b, i: (b, jnp.minimum(i + 1, nblk - 1), 0))
        in_specs = [qspec, prev, cur, nxt, prev, cur, nxt, zspec, zspec]
        args = (q, k, k, k, v, v, v, zk, zv)
    else:
        in_specs = [qspec, zspec, zspec]
        args = (q, zk, zv)
    return pl.pallas_call(
        functools.partial(_attn_kernel, seq_len=L, local=local),
        grid=(nb, nblk),
        in_specs=[pl.BlockSpec(memory_space=pltpu.SMEM)] + in_specs,
        out_specs=qspec,
        out_shape=jax.ShapeDtypeStruct((nb, L, D_Q), BF16),
        compiler_params=_cparams("parallel", "arbitrary"),
        name="attention_local" if local else "attention_ctx",
    )(sink, *args)


def _mix_kernel(*refs, seq_len, aliased):
    if aliased:
        refs = refs[1:]
    (x_ref, a_ref, ap_ref, an_ref, at_ref, u_ref, up_ref, un_ref, gt_ref, g1_ref, sh2_ref, sc2_ref,
     wdw_ref, bdw_ref, lng_ref, lnb_ref, wco_ref, wao_ref, wpool_ref, psc_ref, wpo_ref, wout_ref,
     n2g_ref, rwt_ref, rb_ref, x1_ref, h2_ref, ti_ref, tg_ref, ext_ref, conv_ref, mrg_ref) = refs
    i = pl.program_id(1)
    nt = pl.num_programs(1)
    tm = x_ref.shape[1]
    has_prev = (i > 0).astype(F32)
    has_next = (i < nt - 1).astype(F32)

    def fill_ext(c_ref, p_ref, n_ref):
        ext_ref[0:HALO, :] = p_ref[0] * has_prev
        ext_ref[HALO:HALO + tm, :] = c_ref[0]
        ext_ref[HALO + tm:HALO + tm + HALO, :] = n_ref[0] * has_next

    fill_ext(a_ref, ap_ref, an_ref)
    base = HALO - CONV_K // 2
    rc = CONV_ROW_CHUNK
    for cb in range(D_CONV // LANES):
        sl = slice(cb * LANES, (cb + 1) * LANES)
        for r0 in range(0, tm, rc):
            acc = jnp.zeros((rc, LANES), F32) + bdw_ref[:, sl]
            for ph in range(SUBLANES):
                part = None
                for m in range((base + CONV_K - 1) // SUBLANES + 1):
                    j = m * SUBLANES + ph - base
                    if 0 <= j < CONV_K:
                        term = ext_ref[pl.ds(r0 + m * SUBLANES, rc + SUBLANES), sl] * wdw_ref[j:j + 1, sl]
                        part = term if part is None else part + term
                acc = acc + part[ph:ph + rc]
            conv_ref[r0:r0 + rc, sl] = acc
    acc = conv_ref[...]
    mu = jnp.mean(acc, axis=-1, keepdims=True)
    cen = acc - mu
    var = jnp.mean(cen * cen, axis=-1, keepdims=True)
    ln = cen * lax.rsqrt(var + EPS) * lng_ref[...] + lnb_ref[...]
    act = (ln * _sigmoid(ln)).astype(BF16)

    fill_ext(u_ref, up_ref, un_ref)
    t = i * tm + lax.broadcasted_iota(jnp.int32, (tm, 1), 0)
    u = u_ref[0]
    mixed = []
    for gi, w in enumerate(POOL_WINDOWS):
        sl = slice(gi * POOL_GROUP, (gi + 1) * POOL_GROUP)
        run = ext_ref[:, sl]
        span = 1
        while span < min(w, SUBLANES):
            run = run[:run.shape[0] - span] + run[span:]
            span *= 2
        lo = HALO - w // 2
        tot = run[lo:lo + tm]
        for extra in range(span, w, span):
            tot = tot + run[lo + extra:lo + extra + tm]
        cnt = (jnp.minimum(t + (w - w // 2), seq_len) - jnp.maximum(t - w // 2, 0)).astype(F32)
        pooled = tot / cnt - u[:, sl]
        mixed.append(jnp.dot(pooled.astype(BF16), wpool_ref[gi], preferred_element_type=F32))
    mixed = (jnp.concatenate(mixed, axis=1) * psc_ref[...]).astype(BF16)
    attn = at_ref[0]

    for c0 in range(0, D_MODEL, MERGE_COLS):
        cs = slice(c0, c0 + MERGE_COLS)
        branches = ((act, wco_ref), (attn, wao_ref), (mixed, wpo_ref))
        m = None
        for j, (lhs, w_ref) in enumerate(branches):
            gate = gt_ref[0, :, j * D_MODEL + c0:j * D_MODEL + c0 + MERGE_COLS].astype(F32)
            term = gate * jnp.dot(lhs, w_ref[:, cs], preferred_element_type=F32)
            m = term if m is None else m + term
        mrg_ref[:, cs] = m.astype(BF16)
    x1 = x_ref[0] + g1_ref[0] * jnp.dot(mrg_ref[...], wout_ref[...], preferred_element_type=F32)
    x1_ref[0] = x1

    h2 = _rms(x1, n2g_ref[...]) * (1.0 + sc2_ref[0]) + sh2_ref[0]
    _store_token_rows(h2_ref, (), h2)

    logits = lax.dot_general(rwt_ref[...], h2, (((1,), (1,)), ((), ())), preferred_element_type=F32,
                             precision=lax.Precision.HIGHEST) + rb_ref[...]
    eid = lax.broadcasted_iota(jnp.int32, logits.shape, 0)
    vals = []
    for k in range(TOP_K):
        mx = jnp.max(logits, axis=0, keepdims=True)
        idx = jnp.min(jnp.where(logits == mx, eid, N_EXPERTS), axis=0, keepdims=True)
        ti_ref[0, k:k + 1, :] = idx
        vals.append(mx)
        logits = jnp.where(eid == idx, -jnp.inf, logits)
    es = [jnp.exp(v - vals[0]) for v in vals]
    den = es[0] + es[1] + es[2] + es[3]
    for k in range(TOP_K):
        tg_ref[0, k:k + 1, :] = es[k] / den


def _mix(x, a, attn, u, gates, g1, sh2, sc2, p, h_all, row_off, total_rows):
    nb, L, _ = x.shape
    tm = min(TOKEN_TILE, L)
    nh = tm // HALO
    last_h = L // HALO - 1
    per = L // tm
    tile = lambda w: pl.BlockSpec((1, tm, w), lambda b, i: (b, i, 0))
    prev = pl.BlockSpec((1, HALO, 512), lambda b, i: (b, jnp.maximum(i * nh - 1, 0), 0))
    nxt = pl.BlockSpec((1, HALO, 512), lambda b, i: (b, jnp.minimum((i + 1) * nh, last_h), 0))
    per_b = pl.BlockSpec((1, 1, D_MODEL), lambda b, i: (b, 0, 0))
    full = lambda arr: pl.BlockSpec(arr.shape, lambda b, i: (0,) * arr.ndim)
    route = pl.BlockSpec((1, TOP_K, tm), lambda b, i: (b, 0, i))
    weights = [p["w_dw"], p["b_dw"], p["ln_g"], p["ln_b"], p["w_conv_out"], p["w_attn_out"], p["w_pool"],
               p["pool_scale"], p["w_pool_out"], p["w_out"], p["norm2_g"], p["router_wt"], p["router_b"]]
    in_specs = [tile(D_MODEL), tile(D_CONV), prev, nxt, tile(D_Q), tile(D_POOL), prev, nxt,
                tile(N_BRANCH * D_MODEL), per_b, per_b, per_b] + [full(w) for w in weights]
    args = [x, a, a, a, attn, u, u, u, gates, g1, sh2, sc2, *weights]
    aliases = {}
    if h_all is not None:
        in_specs = [pl.BlockSpec(memory_space=pl.ANY)] + in_specs
        args = [h_all] + args
        aliases = {0: 1}
    return pl.pallas_call(
        functools.partial(_mix_kernel, seq_len=L, aliased=h_all is not None),
        grid=(nb, per),
        in_specs=in_specs,
        out_specs=[tile(D_MODEL),
                   pl.BlockSpec((tm * ROW_CHUNKS, LANES), lambda b, i: (row_off // tm + b * per + i, 0)),
                   route, route],
        out_shape=[jax.ShapeDtypeStruct((nb, L, D_MODEL), F32),
                   jax.ShapeDtypeStruct((total_rows * ROW_CHUNKS, LANES), F32),
                   jax.ShapeDtypeStruct((nb, TOP_K, L), jnp.int32), jax.ShapeDtypeStruct((nb, TOP_K, L), F32)],
        scratch_shapes=[pltpu.VMEM((tm + 2 * HALO, 512), F32), pltpu.VMEM((tm, D_CONV), F32),
                        pltpu.VMEM((tm, D_MODEL), BF16)],
        input_output_aliases=aliases,
        compiler_params=_cparams("parallel", "arbitrary"),
        name="mix",
    )(*args)


def _row_copy(src, src_row8, dst, dst_row8, sem):
    return pltpu.make_async_copy(src.at[pl.ds(src_row8, ROW_CHUNKS)], dst.at[pl.ds(dst_row8, ROW_CHUNKS)], sem)


def _expert_kernel(be_ref, tok_ref, ntok_ref, pslot_ref, slot_ref, h_hbm, wgu_ref, bgu_ref, wd_ref, bd_ref, y_hbm,
                   xbuf, ybuf, xs, wgu_bf, wd_bf, gsem, ssem):
    i = pl.program_id(0)
    n = pl.num_programs(0)
    cur = i % 2
    rows = xs.shape[0]

    def gather(ids_ref, s, wait):
        for r in range(rows):
            src = 0 if wait else pl.multiple_of(ids_ref[0, 0, r], ROW_CHUNKS)
            cp = _row_copy(h_hbm, src, xbuf.at[s], r * ROW_CHUNKS, gsem.at[s])
            cp.wait() if wait else cp.start()

    def scatter(ids_ref, s, wait):
        for r in range(rows):
            dst = 0 if wait else pl.multiple_of(ids_ref[0, 0, r], ROW_CHUNKS)
            cp = _row_copy(ybuf.at[s], r * ROW_CHUNKS, y_hbm, dst, ssem.at[s])
            cp.wait() if wait else cp.start()

    @pl.when(i == 0)
    def _():
        gather(tok_ref, 0, False)
        ybuf[1] = jnp.zeros(ybuf.shape[1:], F32)

    @pl.when(i > 0)
    def _():
        scatter(None, cur, True)

    @pl.when(jnp.logical_or(i == 0, be_ref[i] != be_ref[jnp.maximum(i - 1, 0)]))
    def _():
        wgu_bf[...] = wgu_ref[0, 0].astype(BF16)
        wd_bf[...] = wd_ref[0, 0].astype(BF16)

    gather(None, cur, True)
    xs[...] = _load_token_rows(xbuf, (cur,), rows).astype(BF16)
    gather(ntok_ref, 1 - cur, False)
    scatter(pslot_ref, 1 - cur, False)
    gu = jnp.dot(xs[...], wgu_bf[...], preferred_element_type=F32) + bgu_ref[0, 0]
    g = jnp.minimum(gu[:, :D_EXPERT], SWIGLU_LIMIT)
    up = jnp.clip(gu[:, D_EXPERT:], -SWIGLU_LIMIT, SWIGLU_LIMIT)
    act = (up + 1.0) * (g * _sigmoid(SWIGLU_ALPHA * g))
    y = jnp.dot(act.astype(BF16), wd_bf[...], preferred_element_type=F32) + bd_ref[0, 0]
    _store_token_rows(ybuf, (cur,), y)

    @pl.when(i == n - 1)
    def _():
        scatter(None, 1 - cur, True)
        scatter(slot_ref, cur, False)
        scatter(None, cur, True)
        gather(None, 1 - cur, True)


def _experts(h_flat, block_e, row_tok, row_slot, layer, w_gu, b_gu, w_down, b_down):
    n_blk = block_e.shape[0]
    rows = MOE_ROWS
    P = n_blk * rows
    smem_blk = lambda f: pl.BlockSpec((1, 1, rows), f, memory_space=pltpu.SMEM)
    grid_spec = pltpu.PrefetchScalarGridSpec(
        num_scalar_prefetch=1,
        grid=(n_blk,),
        in_specs=[smem_blk(lambda i, be: (i, 0, 0)),
                  smem_blk(lambda i, be: (jnp.minimum(i + 1, n_blk - 1), 0, 0)),
                  smem_blk(lambda i, be: (i, 0, 0)),
                  smem_blk(lambda i, be: (i + 1, 0, 0)),
                  pl.BlockSpec(memory_space=pl.ANY),
                  pl.BlockSpec((1, 1, D_MODEL, 2 * D_EXPERT), lambda i, be: (layer, be[i], 0, 0)),
                  pl.BlockSpec((1, 1, 1, 2 * D_EXPERT), lambda i, be: (layer, be[i], 0, 0)),
                  pl.BlockSpec((1, 1, D_EXPERT, D_MODEL), lambda i, be: (layer, be[i], 0, 0)),
                  pl.BlockSpec((1, 1, 1, D_MODEL), lambda i, be: (layer, be[i], 0, 0))],
        out_specs=pl.BlockSpec(memory_space=pl.ANY),
        scratch_shapes=[pltpu.VMEM((2, rows * ROW_CHUNKS, LANES), F32),
                        pltpu.VMEM((2, rows * ROW_CHUNKS, LANES), F32),
                        pltpu.VMEM((rows, D_MODEL), BF16), pltpu.VMEM((D_MODEL, 2 * D_EXPERT), BF16), pltpu.VMEM((D_EXPERT, D_MODEL), BF16),
                        pltpu.SemaphoreType.DMA((2,)), pltpu.SemaphoreType.DMA((2,))],
    )
    tok = (row_tok * ROW_CHUNKS).reshape(n_blk, 1, rows)
    slots = (jnp.concatenate([P + jnp.arange(rows, dtype=jnp.int32), row_slot]) * ROW_CHUNKS
             ).reshape(n_blk + 1, 1, rows)
    return pl.pallas_call(
        _expert_kernel,
        grid_spec=grid_spec,
        out_shape=jax.ShapeDtypeStruct(((P + rows) * ROW_CHUNKS, LANES), F32),
        compiler_params=_cparams("arbitrary"),
        name="experts",
    )(block_e, tok, tok, slots, slots, h_flat, w_gu, b_gu[:, :, None, :], w_down, b_down[:, :, None, :])


def _route(ti_flat, tok_of_slot):
    A = ti_flat.shape[0]
    rows = MOE_ROWS
    n_blk = -(-A // rows) + N_EXPERTS
    P = n_blk * rows
    slot_bits = max(A - 1, 1).bit_length()
    keys = jnp.sort(ti_flat * (1 << slot_bits) + jnp.arange(A, dtype=jnp.int32))
    order = keys & ((1 << slot_bits) - 1)
    experts = jnp.arange(N_EXPERTS, dtype=jnp.int32)
    counts = jnp.sum((ti_flat[:, None] == experts[None, :]).astype(jnp.int32), axis=0)
    padded = (counts + rows - 1) // rows * rows
    start = jnp.cumsum(counts) - counts
    padded_end = jnp.cumsum(padded)
    padded_start = padded_end - padded
    blk_row0 = jnp.arange(n_blk, dtype=jnp.int32) * rows
    block_e = jnp.minimum(jnp.sum((padded_end[None, :] <= blk_row0[:, None]).astype(jnp.int32), axis=1),
                          N_EXPERTS - 1).astype(jnp.int32)
    r = jnp.arange(P, dtype=jnp.int32)
    e_of_r = jnp.repeat(block_e, rows)
    j = r - padded_start[e_of_r]
    valid = (j >= 0) & (j < counts[e_of_r])
    slot = order[jnp.clip(start[e_of_r] + j, 0, A - 1)]
    pad_rank = jnp.cumsum(jnp.where(valid, 0, 1).astype(jnp.int32)) - 1
    row_slot = jnp.where(valid, slot, A + pad_rank).astype(jnp.int32)
    row_tok = jnp.where(valid, tok_of_slot[slot], 0).astype(jnp.int32)
    return block_e, row_tok, row_slot


def _slot_tokens(nb, L, row_off):
    b = jnp.arange(nb, dtype=jnp.int32)[:, None, None]
    t = jnp.arange(L, dtype=jnp.int32)[None, None, :]
    return jnp.broadcast_to(row_off + b * L + t, (nb, TOP_K, L)).reshape(-1)


def _final_kernel(x_ref, tg_ref, g2_ref, y0, y1, y2, y3, fg_ref, o_ref):
    o_ref[0] = _rms(_combine_rows(x_ref[0], g2_ref[0], tg_ref[0], (y0, y1, y2, y3)), fg_ref[...])


def _final(x1, y, tg, g2, final_g):
    nb, L, _ = x1.shape
    tm = min(TOKEN_TILE, L)
    tile = lambda w: pl.BlockSpec((1, tm, w), lambda b, i: (b, i, 0))
    return pl.pallas_call(
        _final_kernel,
        grid=(nb, L // tm),
        in_specs=[tile(D_MODEL), tile(TOP_K), pl.BlockSpec((1, 1, D_MODEL), lambda b, i: (b, 0, 0))]
        + _y_specs(tm, L, 0) + [pl.BlockSpec((1, D_MODEL), lambda b, i: (0, 0))],
        out_specs=tile(D_MODEL),
        out_shape=jax.ShapeDtypeStruct((nb, L, D_MODEL), F32),
        compiler_params=_cparams("parallel", "arbitrary"),
        name="final_combine_norm",
    )(x1, tg, g2, y, y, y, y, final_g.reshape(1, D_MODEL))


def _rope_tables(seq_len):
    rows = seq_len // GRID_W
    row = jnp.repeat(jnp.arange(rows), GRID_W).astype(F32)
    col = jnp.tile(jnp.arange(GRID_W), rows).astype(F32)
    inv = ROPE_THETA ** (-jnp.arange(0, ROPE_AXIS_DIM, 2, dtype=F32) / ROPE_AXIS_DIM)
    ang = jnp.concatenate([row[:, None] * inv, col[:, None] * inv], axis=-1)
    cos, sin = jnp.cos(ang), jnp.sin(ang)
    cos_t = jnp.concatenate([cos, cos, cos, cos], axis=-1)
    sin_t = jnp.concatenate([-sin, sin, -sin, sin], axis=-1)
    return cos_t, sin_t


def kernel(x, c, ctx, c_ctx, w_mod, b_mod, norm1_g, w_in, b_gate, w_dw, b_dw, conv_ln_g, conv_ln_b, w_conv_out,
           attn_sink, w_attn_out, w_pool, pool_scale, w_pool_out, w_out, norm2_g, router_w, router_b, w_gu, b_gu,
           w_down, b_down, final_g):
    B, S, D = x.shape
    C = ctx.shape[1]
    T, TZ = B * S, B * C
    depth = w_mod.shape[0]
    cos_x, sin_x = _rope_tables(S)
    cos_z = jnp.ones((C, 2 * HEAD_DIM), F32)
    sin_z = jnp.zeros((C, 2 * HEAD_DIM), F32)
    c_rows = jnp.zeros((16, D), F32).at[:B].set(c).at[B].set(c_ctx)
    z = ctx
    comb_x = comb_z = None
    for l in range(depth):
        last = l == depth - 1
        mod = _modulation(c_rows, w_mod[l], b_mod[l])
        mx = mod[:B].reshape(B, 1, 6, D)
        sh1, sc1, g1, sh2, sc2, g2 = [mx[:, :, j] for j in range(6)]
        mz = jnp.broadcast_to(mod[B].reshape(1, 1, 6, D), (B, 1, 6, D))
        zsh1, zsc1, zg1, zsh2, zsc2, zg2 = [mz[:, :, j] for j in range(6)]
        w_in_bf = w_in[l].astype(BF16)
        p = dict(w_dw=w_dw[l], b_dw=b_dw[l].reshape(1, -1), ln_g=conv_ln_g[l].reshape(1, -1),
                 ln_b=conv_ln_b[l].reshape(1, -1), w_conv_out=w_conv_out[l].astype(BF16),
                 w_attn_out=w_attn_out[l].astype(BF16), w_pool=w_pool[l].astype(BF16),
                 pool_scale=pool_scale[l].reshape(1, -1), w_pool_out=w_pool_out[l].astype(BF16),
                 w_out=w_out[l].astype(BF16), norm2_g=norm2_g[l].reshape(1, -1), router_wt=router_w[l].T,
                 router_b=router_b[l].reshape(-1, 1))

        res = _inproj(x, comb_x, sh1, sc1, norm1_g[l], w_in_bf, b_gate[l], cos_x, sin_x)
        a, q, k, v, u, gates = res[:6]
        x = res[6] if comb_x is not None else x
        zres = _inproj(z, comb_z, zsh1, zsc1, norm1_g[l], w_in_bf, b_gate[l], cos_z, sin_z)
        za, zq, zk, zv, zu, zgates = zres[:6]
        z = zres[6] if comb_z is not None else z

        attn = _attention(q, k, v, zk, zv, attn_sink[l], local=True)
        rows_all = T if last else T + TZ
        x1, h_all, ti, tg = _mix(x, a, attn, u, gates, g1, sh2, sc2, p, None, 0, rows_all)
        ti_flat, tok_of_slot = ti.reshape(-1), _slot_tokens(B, S, 0)
        if not last:
            zattn = _attention(zq, None, None, zk, zv, attn_sink[l], local=False)
            z1, h_all, zti, ztg = _mix(z, za, zattn, zu, zgates, zg1, zsh2, zsc2, p, h_all, T, rows_all)
            ti_flat = jnp.concatenate([ti_flat, zti.reshape(-1)])
            tok_of_slot = jnp.concatenate([tok_of_slot, _slot_tokens(B, C, T)])
        block_e, row_tok, row_slot = _route(ti_flat, tok_of_slot)
        y = _experts(h_all, block_e, row_tok, row_slot, l, w_gu, b_gu, w_down, b_down)
        comb_x = (y, jnp.swapaxes(tg, 1, 2), g2, 0)
        x = x1
        if not last:
            comb_z = (y, jnp.swapaxes(ztg, 1, 2), zg2, T * TOP_K)
            z = z1
    y, tg_rows, g2, _ = comb_x
    return _final(x, y, tg_rows, g2, final_g)
```

```python
import functools

import jax
import jax.numpy as jnp
from jax import lax
from jax.experimental import pallas as pl
from jax.experimental.pallas import tpu as pltpu

F32 = jnp.float32
BF16 = jnp.bfloat16

D_MODEL = 1024
GRID_W = 64
N_HEADS = 8
N_KV_HEADS = 2
GROUP = N_HEADS // N_KV_HEADS
HEAD_DIM = 64
ROPE_AXIS_DIM = HEAD_DIM // 2
ROPE_THETA = 10000.0
WINDOW = 128
ATTN_BLOCK = 128
D_CONV = 512
CONV_K = 31
D_POOL = 512
POOL_WINDOWS = (2, 4, 8, 16)
POOL_GROUP = D_POOL // len(POOL_WINDOWS)
N_BRANCH = 3
D_Q = N_HEADS * HEAD_DIM
D_KV = N_KV_HEADS * HEAD_DIM
N_EXPERTS = 32
TOP_K = 4
D_EXPERT = 1024
SWIGLU_LIMIT = 7.0
SWIGLU_ALPHA = 1.702
EPS = 1e-6
MASK_VALUE = -1e30

O_AVAL, O_AGATE, O_Q, O_K, O_V, O_U, O_GATES = 0, 512, 1024, 1536, 1664, 1792, 2304
D_IN = O_GATES + N_BRANCH * D_MODEL

SUBLANES = 8
LANES = 128
ROW_CHUNKS = D_MODEL // LANES
HALO = 16
TOKEN_TILE = 256
CONV_ROW_CHUNK = 64
MERGE_COLS = 256
MOE_ROWS = 256
DMA_THREADS = 2
VMEM_LIMIT = 56 * 1024 * 1024


def _cparams(*sem):
    return pltpu.CompilerParams(dimension_semantics=sem, vmem_limit_bytes=VMEM_LIMIT)


def _sigmoid(x):
    return 1.0 / (1.0 + jnp.exp(-x))


def _rms(x, g):
    return x * lax.rsqrt(jnp.mean(x * x, axis=-1, keepdims=True) + EPS) * g


def _load_token_rows(ref, lead, rows):
    return jnp.concatenate([ref[(*lead, pl.ds(c, rows, stride=ROW_CHUNKS), slice(None))]
                            for c in range(ROW_CHUNKS)], axis=1)


def _store_token_rows(ref, lead, val):
    rows = val.shape[0]
    for c in range(ROW_CHUNKS):
        ref[(*lead, pl.ds(c, rows, stride=ROW_CHUNKS), slice(None))] = val[:, c * LANES:(c + 1) * LANES]


def _combine_rows(x1, g2, tg, ys):
    tm = x1.shape[0]
    f = tg[:, 0:1] * _load_token_rows(ys[0], (), tm)
    for k in range(1, TOP_K):
        f = f + tg[:, k:k + 1] * _load_token_rows(ys[k], (), tm)
    return x1 + g2 * f


def _mod_kernel(c_ref, w_ref, b_ref, o_ref):
    c = c_ref[...]
    s = c * _sigmoid(c)
    o_ref[...] = jnp.dot(s.astype(BF16), w_ref[...].astype(BF16), preferred_element_type=F32) + b_ref[...]


def _modulation(c_rows, w_mod, b_mod):
    n = w_mod.shape[1]
    tn = 1536
    return pl.pallas_call(
        _mod_kernel,
        grid=(n // tn,),
        in_specs=[pl.BlockSpec((16, D_MODEL), lambda j: (0, 0)),
                  pl.BlockSpec((D_MODEL, tn), lambda j: (0, j)),
                  pl.BlockSpec((1, tn), lambda j: (0, j))],
        out_specs=pl.BlockSpec((16, tn), lambda j: (0, j)),
        out_shape=jax.ShapeDtypeStruct((16, n), F32),
        compiler_params=_cparams("arbitrary"),
        name="modulation",
    )(c_rows, w_mod, b_mod.reshape(1, n))


def _rope(t, cos, sin):
    w = t.shape[1]
    reps = w // cos.shape[1]
    cs = jnp.concatenate([cos] * reps, axis=1) if reps > 1 else cos
    sn = jnp.concatenate([sin] * reps, axis=1) if reps > 1 else sin
    lane = lax.broadcasted_iota(jnp.int32, t.shape, 1)
    first_half = (lane % HEAD_DIM) < (HEAD_DIM // 2)
    partner = jnp.where(first_half, pltpu.roll(t, w - HEAD_DIM // 2, 1), pltpu.roll(t, HEAD_DIM // 2, 1))
    return t * cs + partner * sn


def _inproj_kernel(*refs, combine):
    if combine:
        (x_ref, tg_ref, g2_ref, y0, y1, y2, y3, sh_ref, sc_ref, g_ref, w_ref, bg_ref, cos_ref, sin_ref,
         a_ref, q_ref, k_ref, v_ref, u_ref, gt_ref, xo_ref) = refs
        x = _combine_rows(x_ref[0], g2_ref[0], tg_ref[0], (y0, y1, y2, y3))
        xo_ref[0] = x
    else:
        (x_ref, sh_ref, sc_ref, g_ref, w_ref, bg_ref, cos_ref, sin_ref,
         a_ref, q_ref, k_ref, v_ref, u_ref, gt_ref) = refs
        x = x_ref[0]
    h = _rms(x, g_ref[...]) * (1.0 + sc_ref[0]) + sh_ref[0]
    hb = h.astype(BF16)

    def mm(lo, hi):
        return jnp.dot(hb, w_ref[:, lo:hi], preferred_element_type=F32)

    ag = mm(O_AVAL, O_Q)
    a_ref[0] = ag[:, :D_CONV] * _sigmoid(ag[:, D_CONV:])
    cos = cos_ref[...]
    sin = sin_ref[...]
    q = mm(O_Q, O_K)
    q_ref[0] = (_rope(q, cos, sin) * (HEAD_DIM ** -0.5)).astype(BF16)
    kv = mm(O_K, O_U)
    k_ref[0] = _rope(kv[:, :D_KV], cos, sin).astype(BF16)
    v_ref[0] = kv[:, D_KV:].astype(BF16)
    u_ref[0] = mm(O_U, O_GATES)
    for j in range(N_BRANCH):
        lo = O_GATES + j * D_MODEL
        gl = mm(lo, lo + D_MODEL) + bg_ref[:, j * D_MODEL:(j + 1) * D_MODEL]
        gt_ref[0, :, j * D_MODEL:(j + 1) * D_MODEL] = _sigmoid(gl).astype(BF16)


def _y_specs(tm, L, slot_off):
    per = L // tm
    return [pl.BlockSpec((tm * ROW_CHUNKS, LANES), functools.partial(
        lambda b, i, k: (slot_off // tm + (b * TOP_K + k) * per + i, 0), k=k)) for k in range(TOP_K)]


def _inproj(x, comb, shift, scale, norm_g, w_in_bf, b_gate, cos_t, sin_t):
    nb, L, _ = x.shape
    tm = min(TOKEN_TILE, L)
    full = lambda shp: pl.BlockSpec(shp, lambda b, i: (0,) * len(shp))
    tile = lambda w: pl.BlockSpec((1, tm, w), lambda b, i: (b, i, 0))
    per_b = pl.BlockSpec((1, 1, D_MODEL), lambda b, i: (b, 0, 0))
    outs = [(D_CONV, F32), (D_Q, BF16), (D_KV, BF16), (D_KV, BF16), (D_POOL, F32), (N_BRANCH * D_MODEL, BF16)]
    in_specs = [tile(D_MODEL)]
    args = [x]
    if comb is not None:
        y, tg, g2, slot_off = comb
        in_specs += [tile(TOP_K), per_b] + _y_specs(tm, L, slot_off)
        args += [tg, g2, y, y, y, y]
        outs = outs + [(D_MODEL, F32)]
    in_specs += [per_b, per_b, full((1, D_MODEL)),
                 pl.BlockSpec((D_MODEL, D_IN), lambda b, i: (0, 0), pipeline_mode=pl.Buffered(1)),
                 full((1, N_BRANCH * D_MODEL)),
                 pl.BlockSpec((tm, 2 * HEAD_DIM), lambda b, i: (i, 0)),
                 pl.BlockSpec((tm, 2 * HEAD_DIM), lambda b, i: (i, 0))]
    args += [shift, scale, norm_g.reshape(1, D_MODEL), w_in_bf, b_gate.reshape(1, -1), cos_t, sin_t]
    return pl.pallas_call(
        functools.partial(_inproj_kernel, combine=comb is not None),
        grid=(nb, L // tm),
        in_specs=in_specs,
        out_specs=[tile(w) for w, _ in outs],
        out_shape=[jax.ShapeDtypeStruct((nb, L, w), dt) for w, dt in outs],
        compiler_params=_cparams("parallel", "arbitrary"),
        name="inproj_combine" if comb is not None else "inproj",
    )(*args)


def _attn_kernel(sink_ref, q_ref, *refs, seq_len, local):
    if local:
        kp_ref, kc_ref, kn_ref, vp_ref, vc_ref, vn_ref, zk_ref, zv_ref, o_ref = refs
    else:
        zk_ref, zv_ref, o_ref = refs
    i = pl.program_id(1)
    q = q_ref[0].astype(F32)
    rows = q.shape[0]
    if local:
        k_all = jnp.concatenate([kp_ref[0], kc_ref[0], kn_ref[0], zk_ref[0]], axis=0).astype(F32)
        v_all = jnp.concatenate([vp_ref[0], vc_ref[0], vn_ref[0], zv_ref[0]], axis=0).astype(F32)
        nk = k_all.shape[0]
        qi = lax.broadcasted_iota(jnp.int32, (rows, nk), 0)
        kj = lax.broadcasted_iota(jnp.int32, (rows, nk), 1)
        kpos = i * ATTN_BLOCK - ATTN_BLOCK + kj
        in_band = jnp.where(jnp.abs(kj - ATTN_BLOCK - qi) <= WINDOW, 1, 0)
        in_range = jnp.where(kpos >= 0, jnp.where(kpos < seq_len, 1, 0), 0)
        is_ctx = jnp.where(kj >= 3 * ATTN_BLOCK, 1, 0)
        valid = (in_band * in_range + is_ctx) > 0
        valid = jnp.concatenate([valid] * GROUP, axis=0)
    else:
        k_all = zk_ref[0].astype(F32)
        v_all = zv_ref[0].astype(F32)
    outs = []
    for g in range(N_KV_HEADS):
        kg = k_all[:, g * HEAD_DIM:(g + 1) * HEAD_DIM].astype(BF16)
        vg = v_all[:, g * HEAD_DIM:(g + 1) * HEAD_DIM].astype(BF16)
        heads = [g * GROUP + h for h in range(GROUP)]
        qg = jnp.concatenate([q[:, hh * HEAD_DIM:(hh + 1) * HEAD_DIM] for hh in heads], axis=0).astype(BF16)
        s = lax.dot_general(qg, kg, (((1,), (1,)), ((), ())), preferred_element_type=F32)
        if local:
            s = jnp.where(valid, s, MASK_VALUE)
        sink = jnp.concatenate([jnp.full((rows, 1), sink_ref[hh], F32) for hh in heads], axis=0)
        m = jnp.maximum(jnp.max(s, axis=-1, keepdims=True), sink)
        p = jnp.exp(s - m)
        den = jnp.sum(p, axis=-1, keepdims=True) + jnp.exp(sink - m)
        o = jnp.dot(p.astype(BF16), vg, preferred_element_type=F32) / den
        outs += [o[h * rows:(h + 1) * rows] for h in range(GROUP)]
    o_ref[0] = jnp.concatenate(outs, axis=1).astype(BF16)


def _attention(q, k, v, zk, zv, sink, local):
    nb, L, _ = q.shape
    C = zk.shape[1]
    tq = ATTN_BLOCK if local else L
    nblk = L // tq
    qspec = pl.BlockSpec((1, tq, D_Q), lambda b, i: (b, i, 0))
    zspec = pl.BlockSpec((1, C, D_KV), lambda b, i: (b, 0, 0))
    if local:
        prev = pl.BlockSpec((1, tq, D_KV), lambda b, i: (b, jnp.maximum(i - 1, 0), 0))
        cur = pl.BlockSpec((1, tq, D_KV), lambda b, i: (b, i, 0))
        nxt = pl.BlockSpec((1, tq, D_KV), lambda b, i: (b, jnp.minimum(i + 1, nblk - 1), 0))
        in_specs = [qspec, prev, cur, nxt, prev, cur, nxt, zspec, zspec]
        args = (q, k, k, k, v, v, v, zk, zv)
    else:
        in_specs = [qspec, zspec, zspec]
        args = (q, zk, zv)
    return pl.pallas_call(
        functools.partial(_attn_kernel, seq_len=L, local=local),
        grid=(nb, nblk),
        in_specs=[pl.BlockSpec(memory_space=pltpu.SMEM)] + in_specs,
        out_specs=qspec,
        out_shape=jax.ShapeDtypeStruct((nb, L, D_Q), BF16),
        compiler_params=_cparams("parallel", "arbitrary"),
        name="attention_local" if local else "attention_ctx",
    )(sink, *args)


def _mix_kernel(*refs, seq_len, aliased):
    if aliased:
        refs = refs[1:]
    (x_ref, a_ref, ap_ref, an_ref, at_ref, u_ref, up_ref, un_ref, gt_ref, g1_ref, sh2_ref, sc2_ref,
     wdw_ref, bdw_ref, lng_ref, lnb_ref, wco_ref, wao_ref, wpool_ref, psc_ref, wpo_ref, wout_ref,
     n2g_ref, rwt_ref, rb_ref, x1_ref, h2_ref, ti_ref, tg_ref, ext_ref, conv_ref, mrg_ref) = refs
    i = pl.program_id(1)
    nt = pl.num_programs(1)
    tm = x_ref.shape[1]
    has_prev = (i > 0).astype(F32)
    has_next = (i < nt - 1).astype(F32)

    def fill_ext(c_ref, p_ref, n_ref):
        ext_ref[0:HALO, :] = p_ref[0] * has_prev
        ext_ref[HALO:HALO + tm, :] = c_ref[0]
        ext_ref[HALO + tm:HALO + tm + HALO, :] = n_ref[0] * has_next

    fill_ext(a_ref, ap_ref, an_ref)
    base = HALO - CONV_K // 2
    rc = CONV_ROW_CHUNK
    for cb in range(D_CONV // LANES):
        sl = slice(cb * LANES, (cb + 1) * LANES)
        for r0 in range(0, tm, rc):
            acc = jnp.zeros((rc, LANES), F32) + bdw_ref[:, sl]
            for ph in range(SUBLANES):
                part = None
                for m in range((base + CONV_K - 1) // SUBLANES + 1):
                    j = m * SUBLANES + ph - base
                    if 0 <= j < CONV_K:
                        term = ext_ref[pl.ds(r0 + m * SUBLANES, rc + SUBLANES), sl] * wdw_ref[j:j + 1, sl]
                        part = term if part is None else part + term
                acc = acc + part[ph:ph + rc]
            conv_ref[r0:r0 + rc, sl] = acc
    acc = conv_ref[...]
    mu = jnp.mean(acc, axis=-1, keepdims=True)
    cen = acc - mu
    var = jnp.mean(cen * cen, axis=-1, keepdims=True)
    ln = cen * lax.rsqrt(var + EPS) * lng_ref[...] + lnb_ref[...]
    act = (ln * _sigmoid(ln)).astype(BF16)

    fill_ext(u_ref, up_ref, un_ref)
    t = i * tm + lax.broadcasted_iota(jnp.int32, (tm, 1), 0)
    u = u_ref[0]
    mixed = []
    for gi, w in enumerate(POOL_WINDOWS):
        sl = slice(gi * POOL_GROUP, (gi + 1) * POOL_GROUP)
        run = ext_ref[:, sl]
        span = 1
        while span < min(w, SUBLANES):
            run = run[:run.shape[0] - span] + run[span:]
            span *= 2
        lo = HALO - w // 2
        tot = run[lo:lo + tm]
        for extra in range(span, w, span):
            tot = tot + run[lo + extra:lo + extra + tm]
        cnt = (jnp.minimum(t + (w - w // 2), seq_len) - jnp.maximum(t - w // 2, 0)).astype(F32)
        pooled = tot / cnt - u[:, sl]
        mixed.append(jnp.dot(pooled.astype(BF16), wpool_ref[gi], preferred_element_type=F32))
    mixed = (jnp.concatenate(mixed, axis=1) * psc_ref[...]).astype(BF16)
    attn = at_ref[0]

    for c0 in range(0, D_MODEL, MERGE_COLS):
        cs = slice(c0, c0 + MERGE_COLS)
        branches = ((act, wco_ref), (attn, wao_ref), (mixed, wpo_ref))
        m = None
        for j, (lhs, w_ref) in enumerate(branches):
            gate = gt_ref[0, :, j * D_MODEL + c0:j * D_MODEL + c0 + MERGE_COLS].astype(F32)
            term = gate * jnp.dot(lhs, w_ref[:, cs], preferred_element_type=F32)
            m = term if m is None else m + term
        mrg_ref[:, cs] = m.astype(BF16)
    x1 = x_ref[0] + g1_ref[0] * jnp.dot(mrg_ref[...], wout_ref[...], preferred_element_type=F32)
    x1_ref[0] = x1

    h2 = _rms(x1, n2g_ref[...]) * (1.0 + sc2_ref[0]) + sh2_ref[0]
    _store_token_rows(h2_ref, (), h2)

    logits = lax.dot_general(rwt_ref[...], h2, (((1,), (1,)), ((), ())), preferred_element_type=F32,
                             precision=lax.Precision.HIGHEST) + rb_ref[...]
    eid = lax.broadcasted_iota(jnp.int32, logits.shape, 0)
    vals = []
    for k in range(TOP_K):
        mx = jnp.max(logits, axis=0, keepdims=True)
        idx = jnp.min(jnp.where(logits == mx, eid, N_EXPERTS), axis=0, keepdims=True)
        ti_ref[0, k:k + 1, :] = idx
        vals.append(mx)
        logits = jnp.where(eid == idx, -jnp.inf, logits)
    es = [jnp.exp(v - vals[0]) for v in vals]
    den = es[0] + es[1] + es[2] + es[3]
    for k in range(TOP_K):
        tg_ref[0, k:k + 1, :] = es[k] / den


def _mix(x, a, attn, u, gates, g1, sh2, sc2, p, h_all, row_off, total_rows):
    nb, L, _ = x.shape
    tm = min(TOKEN_TILE, L)
    nh = tm // HALO
    last_h = L // HALO - 1
    per = L // tm
    tile = lambda w: pl.BlockSpec((1, tm, w), lambda b, i: (b, i, 0))
    prev = pl.BlockSpec((1, HALO, 512), lambda b, i: (b, jnp.maximum(i * nh - 1, 0), 0))
    nxt = pl.BlockSpec((1, HALO, 512), lambda b, i: (b, jnp.minimum((i + 1) * nh, last_h), 0))
    per_b = pl.BlockSpec((1, 1, D_MODEL), lambda b, i: (b, 0, 0))
    full = lambda arr: pl.BlockSpec(arr.shape, lambda b, i: (0,) * arr.ndim)
    route = pl.BlockSpec((1, TOP_K, tm), lambda b, i: (b, 0, i))
    weights = [p["w_dw"], p["b_dw"], p["ln_g"], p["ln_b"], p["w_conv_out"], p["w_attn_out"], p["w_pool"],
               p["pool_scale"], p["w_pool_out"], p["w_out"], p["norm2_g"], p["router_wt"], p["router_b"]]
    in_specs = [tile(D_MODEL), tile(D_CONV), prev, nxt, tile(D_Q), tile(D_POOL), prev, nxt,
                tile(N_BRANCH * D_MODEL), per_b, per_b, per_b] + [full(w) for w in weights]
    args = [x, a, a, a, attn, u, u, u, gates, g1, sh2, sc2, *weights]
    aliases = {}
    if h_all is not None:
        in_specs = [pl.BlockSpec(memory_space=pl.ANY)] + in_specs
        args = [h_all] + args
        aliases = {0: 1}
    return pl.pallas_call(
        functools.partial(_mix_kernel, seq_len=L, aliased=h_all is not None),
        grid=(nb, per),
        in_specs=in_specs,
        out_specs=[tile(D_MODEL),
                   pl.BlockSpec((tm * ROW_CHUNKS, LANES), lambda b, i: (row_off // tm + b * per + i, 0)),
                   route, route],
        out_shape=[jax.ShapeDtypeStruct((nb, L, D_MODEL), F32),
                   jax.ShapeDtypeStruct((total_rows * ROW_CHUNKS, LANES), F32),
                   jax.ShapeDtypeStruct((nb, TOP_K, L), jnp.int32), jax.ShapeDtypeStruct((nb, TOP_K, L), F32)],
        scratch_shapes=[pltpu.VMEM((tm + 2 * HALO, 512), F32), pltpu.VMEM((tm, D_CONV), F32),
                        pltpu.VMEM((tm, D_MODEL), BF16)],
        input_output_aliases=aliases,
        compiler_params=_cparams("parallel", "arbitrary"),
        name="mix",
    )(*args)


def _row_copy(src, src_row8, dst, dst_row8, sem):
    return pltpu.make_async_copy(src.at[pl.ds(src_row8, ROW_CHUNKS)], dst.at[pl.ds(dst_row8, ROW_CHUNKS)], sem)


def _expert_kernel(be_ref, tok_ref, ntok_ref, pslot_ref, slot_ref, h_hbm, wgu_ref, bgu_ref, wd_ref, bd_ref, y_hbm,
                   xbuf, ybuf, xs, wgu_bf, wd_bf, gsem, ssem):
    i = pl.program_id(0)
    n = pl.num_programs(0)
    cur = i % 2
    rows = xs.shape[0]

    def gather(ids_ref, s, wait):
        for r in range(rows):
            src = 0 if wait else pl.multiple_of(ids_ref[0, 0, r], ROW_CHUNKS)
            cp = _row_copy(h_hbm, src, xbuf.at[s], r * ROW_CHUNKS, gsem.at[s])
            cp.wait() if wait else cp.start(priority=r % DMA_THREADS)

    def scatter(ids_ref, s, wait):
        for r in range(rows):
            dst = 0 if wait else pl.multiple_of(ids_ref[0, 0, r], ROW_CHUNKS)
            cp = _row_copy(ybuf.at[s], r * ROW_CHUNKS, y_hbm, dst, ssem.at[s])
            cp.wait() if wait else cp.start(priority=r % DMA_THREADS)

    @pl.when(i == 0)
    def _():
        gather(tok_ref, 0, False)
        ybuf[1] = jnp.zeros(ybuf.shape[1:], F32)

    @pl.when(i > 0)
    def _():
        scatter(None, cur, True)

    @pl.when(jnp.logical_or(i == 0, be_ref[i] != be_ref[jnp.maximum(i - 1, 0)]))
    def _():
        wgu_bf[...] = wgu_ref[0, 0].astype(BF16)
        wd_bf[...] = wd_ref[0, 0].astype(BF16)

    gather(None, cur, True)
    xs[...] = _load_token_rows(xbuf, (cur,), rows).astype(BF16)
    gather(ntok_ref, 1 - cur, False)
    scatter(pslot_ref, 1 - cur, False)
    gu = jnp.dot(xs[...], wgu_bf[...], preferred_element_type=F32) + bgu_ref[0, 0]
    g = jnp.minimum(gu[:, :D_EXPERT], SWIGLU_LIMIT)
    up = jnp.clip(gu[:, D_EXPERT:], -SWIGLU_LIMIT, SWIGLU_LIMIT)
    act = (up + 1.0) * (g * _sigmoid(SWIGLU_ALPHA * g))
    y = jnp.dot(act.astype(BF16), wd_bf[...], preferred_element_type=F32) + bd_ref[0, 0]
    _store_token_rows(ybuf, (cur,), y)

    @pl.when(i == n - 1)
    def _():
        scatter(None, 1 - cur, True)
        scatter(slot_ref, cur, False)
        scatter(None, cur, True)
        gather(None, 1 - cur, True)


def _experts(h_flat, block_e, row_tok, row_slot, layer, w_gu, b_gu, w_down, b_down):
    n_blk = block_e.shape[0]
    rows = MOE_ROWS
    P = n_blk * rows
    smem_blk = lambda f: pl.BlockSpec((1, 1, rows), f, memory_space=pltpu.SMEM)
    grid_spec = pltpu.PrefetchScalarGridSpec(
        num_scalar_prefetch=1,
        grid=(n_blk,),
        in_specs=[smem_blk(lambda i, be: (i, 0, 0)),
                  smem_blk(lambda i, be: (jnp.minimum(i + 1, n_blk - 1), 0, 0)),
                  smem_blk(lambda i, be: (i, 0, 0)),
                  smem_blk(lambda i, be: (i + 1, 0, 0)),
                  pl.BlockSpec(memory_space=pl.ANY),
                  pl.BlockSpec((1, 1, D_MODEL, 2 * D_EXPERT), lambda i, be: (layer, be[i], 0, 0)),
                  pl.BlockSpec((1, 1, 1, 2 * D_EXPERT), lambda i, be: (layer, be[i], 0, 0)),
                  pl.BlockSpec((1, 1, D_EXPERT, D_MODEL), lambda i, be: (layer, be[i], 0, 0)),
                  pl.BlockSpec((1, 1, 1, D_MODEL), lambda i, be: (layer, be[i], 0, 0))],
        out_specs=pl.BlockSpec(memory_space=pl.ANY),
        scratch_shapes=[pltpu.VMEM((2, rows * ROW_CHUNKS, LANES), F32),
                        pltpu.VMEM((2, rows * ROW_CHUNKS, LANES), F32),
                        pltpu.VMEM((rows, D_MODEL), BF16), pltpu.VMEM((D_MODEL, 2 * D_EXPERT), BF16), pltpu.VMEM((D_EXPERT, D_MODEL), BF16),
                        pltpu.SemaphoreType.DMA((2,)), pltpu.SemaphoreType.DMA((2,))],
    )
    tok = (row_tok * ROW_CHUNKS).reshape(n_blk, 1, rows)
    slots = (jnp.concatenate([P + jnp.arange(rows, dtype=jnp.int32), row_slot]) * ROW_CHUNKS
             ).reshape(n_blk + 1, 1, rows)
    return pl.pallas_call(
        _expert_kernel,
        grid_spec=grid_spec,
        out_shape=jax.ShapeDtypeStruct(((P + rows) * ROW_CHUNKS, LANES), F32),
        compiler_params=_cparams("arbitrary"),
        name="experts",
    )(block_e, tok, tok, slots, slots, h_flat, w_gu, b_gu[:, :, None, :], w_down, b_down[:, :, None, :])


def _route(ti_flat, tok_of_slot):
    A = ti_flat.shape[0]
    rows = MOE_ROWS
    n_blk = -(-A // rows) + N_EXPERTS
    P = n_blk * rows
    slot_bits = max(A - 1, 1).bit_length()
    keys = jnp.sort(ti_flat * (1 << slot_bits) + jnp.arange(A, dtype=jnp.int32))
    order = keys & ((1 << slot_bits) - 1)
    experts = jnp.arange(N_EXPERTS, dtype=jnp.int32)
    counts = jnp.sum((ti_flat[:, None] == experts[None, :]).astype(jnp.int32), axis=0)
    padded = (counts + rows - 1) // rows * rows
    start = jnp.cumsum(counts) - counts
    padded_end = jnp.cumsum(padded)
    padded_start = padded_end - padded
    blk_row0 = jnp.arange(n_blk, dtype=jnp.int32) * rows
    block_e = jnp.minimum(jnp.sum((padded_end[None, :] <= blk_row0[:, None]).astype(jnp.int32), axis=1),
                          N_EXPERTS - 1).astype(jnp.int32)
    r = jnp.arange(P, dtype=jnp.int32)
    e_of_r = jnp.repeat(block_e, rows)
    j = r - padded_start[e_of_r]
    valid = (j >= 0) & (j < counts[e_of_r])
    slot = order[jnp.clip(start[e_of_r] + j, 0, A - 1)]
    pad_rank = jnp.cumsum(jnp.where(valid, 0, 1).astype(jnp.int32)) - 1
    row_slot = jnp.where(valid, slot, A + pad_rank).astype(jnp.int32)
    row_tok = jnp.where(valid, tok_of_slot[slot], 0).astype(jnp.int32)
    return block_e, row_tok, row_slot


def _slot_tokens(nb, L, row_off):
    b = jnp.arange(nb, dtype=jnp.int32)[:, None, None]
    t = jnp.arange(L, dtype=jnp.int32)[None, None, :]
    return jnp.broadcast_to(row_off + b * L + t, (nb, TOP_K, L)).reshape(-1)


def _final_kernel(x_ref, tg_ref, g2_ref, y0, y1, y2, y3, fg_ref, o_ref):
    o_ref[0] = _rms(_combine_rows(x_ref[0], g2_ref[0], tg_ref[0], (y0, y1, y2, y3)), fg_ref[...])


def _final(x1, y, tg, g2, final_g):
    nb, L, _ = x1.shape
    tm = min(TOKEN_TILE, L)
    tile = lambda w: pl.BlockSpec((1, tm, w), lambda b, i: (b, i, 0))
    return pl.pallas_call(
        _final_kernel,
        grid=(nb, L // tm),
        in_specs=[tile(D_MODEL), tile(TOP_K), pl.BlockSpec((1, 1, D_MODEL), lambda b, i: (b, 0, 0))]
        + _y_specs(tm, L, 0) + [pl.BlockSpec((1, D_MODEL), lambda b, i: (0, 0))],
        out_specs=tile(D_MODEL),
        out_shape=jax.ShapeDtypeStruct((nb, L, D_MODEL), F32),
        compiler_params=_cparams("parallel", "arbitrary"),
        name="final_combine_norm",
    )(x1, tg, g2, y, y, y, y, final_g.reshape(1, D_MODEL))


def _rope_tables(seq_len):
    rows = seq_len // GRID_W
    row = jnp.repeat(jnp.arange(rows), GRID_W).astype(F32)
    col = jnp.tile(jnp.arange(GRID_W), rows).astype(F32)
    inv = ROPE_THETA ** (-jnp.arange(0, ROPE_AXIS_DIM, 2, dtype=F32) / ROPE_AXIS_DIM)
    ang = jnp.concatenate([row[:, None] * inv, col[:, None] * inv], axis=-1)
    cos, sin = jnp.cos(ang), jnp.sin(ang)
    cos_t = jnp.concatenate([cos, cos, cos, cos], axis=-1)
    sin_t = jnp.concatenate([-sin, sin, -sin, sin], axis=-1)
    return cos_t, sin_t


def kernel(x, c, ctx, c_ctx, w_mod, b_mod, norm1_g, w_in, b_gate, w_dw, b_dw, conv_ln_g, conv_ln_b, w_conv_out,
           attn_sink, w_attn_out, w_pool, pool_scale, w_pool_out, w_out, norm2_g, router_w, router_b, w_gu, b_gu,
           w_down, b_down, final_g):
    B, S, D = x.shape
    C = ctx.shape[1]
    T, TZ = B * S, B * C
    depth = w_mod.shape[0]
    cos_x, sin_x = _rope_tables(S)
    cos_z = jnp.ones((C, 2 * HEAD_DIM), F32)
    sin_z = jnp.zeros((C, 2 * HEAD_DIM), F32)
    c_rows = jnp.zeros((16, D), F32).at[:B].set(c).at[B].set(c_ctx)
    z = ctx
    comb_x = comb_z = None
    for l in range(depth):
        last = l == depth - 1
        mod = _modulation(c_rows, w_mod[l], b_mod[l])
        mx = mod[:B].reshape(B, 1, 6, D)
        sh1, sc1, g1, sh2, sc2, g2 = [mx[:, :, j] for j in range(6)]
        mz = jnp.broadcast_to(mod[B].reshape(1, 1, 6, D), (B, 1, 6, D))
        zsh1, zsc1, zg1, zsh2, zsc2, zg2 = [mz[:, :, j] for j in range(6)]
        w_in_bf = w_in[l].astype(BF16)
        p = dict(w_dw=w_dw[l], b_dw=b_dw[l].reshape(1, -1), ln_g=conv_ln_g[l].reshape(1, -1),
                 ln_b=conv_ln_b[l].reshape(1, -1), w_conv_out=w_conv_out[l].astype(BF16),
                 w_attn_out=w_attn_out[l].astype(BF16), w_pool=w_pool[l].astype(BF16),
                 pool_scale=pool_scale[l].reshape(1, -1), w_pool_out=w_pool_out[l].astype(BF16),
                 w_out=w_out[l].astype(BF16), norm2_g=norm2_g[l].reshape(1, -1), router_wt=router_w[l].T,
                 router_b=router_b[l].reshape(-1, 1))

        res = _inproj(x, comb_x, sh1, sc1, norm1_g[l], w_in_bf, b_gate[l], cos_x, sin_x)
        a, q, k, v, u, gates = res[:6]
        x = res[6] if comb_x is not None else x
        zres = _inproj(z, comb_z, zsh1, zsc1, norm1_g[l], w_in_bf, b_gate[l], cos_z, sin_z)
        za, zq, zk, zv, zu, zgates = zres[:6]
        z = zres[6] if comb_z is not None else z

        attn = _attention(q, k, v, zk, zv, attn_sink[l], local=True)
        rows_all = T if last else T + TZ
        x1, h_all, ti, tg = _mix(x, a, attn, u, gates, g1, sh2, sc2, p, None, 0, rows_all)
        ti_flat, tok_of_slot = ti.reshape(-1), _slot_tokens(B, S, 0)
        if not last:
            zattn = _attention(zq, None, None, zk, zv, attn_sink[l], local=False)
            z1, h_all, zti, ztg = _mix(z, za, zattn, zu, zgates, zg1, zsh2, zsc2, p, h_all, T, rows_all)
            ti_flat = jnp.concatenate([ti_flat, zti.reshape(-1)])
            tok_of_slot = jnp.concatenate([tok_of_slot, _slot_tokens(B, C, T)])
        block_e, row_tok, row_slot = _route(ti_flat, tok_of_slot)
        y = _experts(h_all, block_e, row_tok, row_slot, l, w_gu, b_gu, w_down, b_down)
        comb_x = (y, jnp.swapaxes(tg, 1, 2), g2, 0)
        x = x1
        if not last:
            comb_z = (y, jnp.swapaxes(ztg, 1, 2), zg2, T * TOP_K)
            z = z1
    y, tg_rows, g2, _ = comb_x
    return _final(x, y, tg_rows, g2, final_g)
```

```python
import functools

import jax
import jax.numpy as jnp
from jax import lax
from jax.experimental import pallas as pl
from jax.experimental.pallas import tpu as pltpu

F32 = jnp.float32
BF16 = jnp.bfloat16

D_MODEL = 1024
GRID_W = 64
N_HEADS = 8
N_KV_HEADS = 2
GROUP = N_HEADS // N_KV_HEADS
HEAD_DIM = 64
ROPE_AXIS_DIM = HEAD_DIM // 2
ROPE_THETA = 10000.0
WINDOW = 128
ATTN_BLOCK = 128
ATTN_STEP_BLOCKS = 4
D_CONV = 512
CONV_K = 31
D_POOL = 512
POOL_WINDOWS = (2, 4, 8, 16)
POOL_GROUP = D_POOL // len(POOL_WINDOWS)
N_BRANCH = 3
D_Q = N_HEADS * HEAD_DIM
D_KV = N_KV_HEADS * HEAD_DIM
N_EXPERTS = 32
TOP_K = 4
D_EXPERT = 1024
SWIGLU_LIMIT = 7.0
SWIGLU_ALPHA = 1.702
EPS = 1e-6
MASK_VALUE = -1e30

O_AVAL, O_AGATE, O_Q, O_K, O_V, O_U, O_GATES = 0, 512, 1024, 1536, 1664, 1792, 2304
D_IN = O_GATES + N_BRANCH * D_MODEL

SUBLANES = 8
LANES = 128
ROW_CHUNKS = D_MODEL // LANES
HALO = 16
TOKEN_TILE = 256
CONV_ROW_CHUNK = 64
MERGE_COLS = 256
MOE_ROWS = 256
DMA_THREADS = 2
VMEM_LIMIT = 56 * 1024 * 1024


def _cparams(*sem):
    return pltpu.CompilerParams(dimension_semantics=sem, vmem_limit_bytes=VMEM_LIMIT)


def _sigmoid(x):
    return 1.0 / (1.0 + jnp.exp(-x))


def _rms(x, g):
    return x * lax.rsqrt(jnp.mean(x * x, axis=-1, keepdims=True) + EPS) * g


def _load_token_rows(ref, lead, rows):
    return jnp.concatenate([ref[(*lead, pl.ds(c, rows, stride=ROW_CHUNKS), slice(None))]
                            for c in range(ROW_CHUNKS)], axis=1)


def _store_token_rows(ref, lead, val):
    rows = val.shape[0]
    for c in range(ROW_CHUNKS):
        ref[(*lead, pl.ds(c, rows, stride=ROW_CHUNKS), slice(None))] = val[:, c * LANES:(c + 1) * LANES]


def _combine_rows(x1, g2, tg, ys):
    tm = x1.shape[0]
    f = tg[:, 0:1] * _load_token_rows(ys[0], (), tm)
    for k in range(1, TOP_K):
        f = f + tg[:, k:k + 1] * _load_token_rows(ys[k], (), tm)
    return x1 + g2 * f


def _mod_kernel(c_ref, w_ref, b_ref, o_ref):
    c = c_ref[...]
    s = c * _sigmoid(c)
    o_ref[...] = jnp.dot(s.astype(BF16), w_ref[...].astype(BF16), preferred_element_type=F32) + b_ref[...]


def _modulation(c_rows, w_mod, b_mod):
    n = w_mod.shape[1]
    tn = 1536
    return pl.pallas_call(
        _mod_kernel,
        grid=(n // tn,),
        in_specs=[pl.BlockSpec((16, D_MODEL), lambda j: (0, 0)),
                  pl.BlockSpec((D_MODEL, tn), lambda j: (0, j)),
                  pl.BlockSpec((1, tn), lambda j: (0, j))],
        out_specs=pl.BlockSpec((16, tn), lambda j: (0, j)),
        out_shape=jax.ShapeDtypeStruct((16, n), F32),
        compiler_params=_cparams("arbitrary"),
        name="modulation",
    )(c_rows, w_mod, b_mod.reshape(1, n))


def _rope(t, cos, sin):
    w = t.shape[1]
    reps = w // cos.shape[1]
    cs = jnp.concatenate([cos] * reps, axis=1) if reps > 1 else cos
    sn = jnp.concatenate([sin] * reps, axis=1) if reps > 1 else sin
    lane = lax.broadcasted_iota(jnp.int32, t.shape, 1)
    first_half = (lane % HEAD_DIM) < (HEAD_DIM // 2)
    partner = jnp.where(first_half, pltpu.roll(t, w - HEAD_DIM // 2, 1), pltpu.roll(t, HEAD_DIM // 2, 1))
    return t * cs + partner * sn


def _inproj_kernel(*refs, combine):
    if combine:
        (x_ref, tg_ref, g2_ref, y0, y1, y2, y3, sh_ref, sc_ref, g_ref, w_ref, bg_ref, cos_ref, sin_ref,
         a_ref, q_ref, k_ref, v_ref, u_ref, gt_ref, xo_ref) = refs
        x = _combine_rows(x_ref[0], g2_ref[0], tg_ref[0], (y0, y1, y2, y3))
        xo_ref[0] = x
    else:
        (x_ref, sh_ref, sc_ref, g_ref, w_ref, bg_ref, cos_ref, sin_ref,
         a_ref, q_ref, k_ref, v_ref, u_ref, gt_ref) = refs
        x = x_ref[0]
    h = _rms(x, g_ref[...]) * (1.0 + sc_ref[0]) + sh_ref[0]
    hb = h.astype(BF16)

    def mm(lo, hi):
        return jnp.dot(hb, w_ref[:, lo:hi], preferred_element_type=F32)

    ag = mm(O_AVAL, O_Q)
    a_ref[0] = ag[:, :D_CONV] * _sigmoid(ag[:, D_CONV:])
    cos = cos_ref[...]
    sin = sin_ref[...]
    q = mm(O_Q, O_K)
    q_ref[0] = (_rope(q, cos, sin) * (HEAD_DIM ** -0.5)).astype(BF16)
    kv = mm(O_K, O_U)
    k_ref[0] = _rope(kv[:, :D_KV], cos, sin).astype(BF16)
    v_ref[0] = kv[:, D_KV:].astype(BF16)
    u_ref[0] = mm(O_U, O_GATES)
    for j in range(N_BRANCH):
        lo = O_GATES + j * D_MODEL
        gl = mm(lo, lo + D_MODEL) + bg_ref[:, j * D_MODEL:(j + 1) * D_MODEL]
        gt_ref[0, :, j * D_MODEL:(j + 1) * D_MODEL] = _sigmoid(gl).astype(BF16)


def _y_specs(tm, L, slot_off):
    per = L // tm
    return [pl.BlockSpec((tm * ROW_CHUNKS, LANES), functools.partial(
        lambda b, i, k: (slot_off // tm + (b * TOP_K + k) * per + i, 0), k=k)) for k in range(TOP_K)]


def _inproj(x, comb, shift, scale, norm_g, w_in_bf, b_gate, cos_t, sin_t):
    nb, L, _ = x.shape
    tm = min(TOKEN_TILE, L)
    full = lambda shp: pl.BlockSpec(shp, lambda b, i: (0,) * len(shp))
    tile = lambda w: pl.BlockSpec((1, tm, w), lambda b, i: (b, i, 0))
    per_b = pl.BlockSpec((1, 1, D_MODEL), lambda b, i: (b, 0, 0))
    outs = [(D_CONV, F32), (D_Q, BF16), (D_KV, BF16), (D_KV, BF16), (D_POOL, F32), (N_BRANCH * D_MODEL, BF16)]
    in_specs = [tile(D_MODEL)]
    args = [x]
    if comb is not None:
        y, tg, g2, slot_off = comb
        in_specs += [tile(TOP_K), per_b] + _y_specs(tm, L, slot_off)
        args += [tg, g2, y, y, y, y]
        outs = outs + [(D_MODEL, F32)]
    in_specs += [per_b, per_b, full((1, D_MODEL)),
                 pl.BlockSpec((D_MODEL, D_IN), lambda b, i: (0, 0), pipeline_mode=pl.Buffered(1)),
                 full((1, N_BRANCH * D_MODEL)),
                 pl.BlockSpec((tm, 2 * HEAD_DIM), lambda b, i: (i, 0)),
                 pl.BlockSpec((tm, 2 * HEAD_DIM), lambda b, i: (i, 0))]
    args += [shift, scale, norm_g.reshape(1, D_MODEL), w_in_bf, b_gate.reshape(1, -1), cos_t, sin_t]
    return pl.pallas_call(
        functools.partial(_inproj_kernel, combine=comb is not None),
        grid=(nb, L // tm),
        in_specs=in_specs,
        out_specs=[tile(w) for w, _ in outs],
        out_shape=[jax.ShapeDtypeStruct((nb, L, w), dt) for w, dt in outs],
        compiler_params=_cparams("parallel", "arbitrary"),
        name="inproj_combine" if comb is not None else "inproj",
    )(*args)


def _attn_kernel(sink_ref, q_ref, *refs, seq_len, local):
    if local:
        kp_ref, kc_ref, kn_ref, vp_ref, vc_ref, vn_ref, zk_ref, zv_ref, o_ref = refs
        k_win = jnp.concatenate([kp_ref[0], kc_ref[0], kn_ref[0]], axis=0).astype(F32)
        v_win = jnp.concatenate([vp_ref[0], vc_ref[0], vn_ref[0]], axis=0).astype(F32)
        n_sub = q_ref.shape[1] // ATTN_BLOCK
        rows = ATTN_BLOCK
    else:
        zk_ref, zv_ref, o_ref = refs
        n_sub = 1
        rows = q_ref.shape[1]
    i = pl.program_id(1)
    zk = zk_ref[0].astype(F32)
    zv = zv_ref[0].astype(F32)
    n_loc = 3 * ATTN_BLOCK
    for j in range(n_sub):
        q = q_ref[0, j * rows:(j + 1) * rows, :].astype(F32)
        if local:
            k_all = jnp.concatenate([k_win[j * ATTN_BLOCK:j * ATTN_BLOCK + n_loc], zk], axis=0)
            v_all = jnp.concatenate([v_win[j * ATTN_BLOCK:j * ATTN_BLOCK + n_loc], zv], axis=0)
            nk = k_all.shape[0]
            qi = lax.broadcasted_iota(jnp.int32, (rows, nk), 0)
            kj = lax.broadcasted_iota(jnp.int32, (rows, nk), 1)
            kpos = (i * n_sub + j) * ATTN_BLOCK - ATTN_BLOCK + kj
            in_band = jnp.where(jnp.abs(kj - ATTN_BLOCK - qi) <= WINDOW, 1, 0)
            in_range = jnp.where(kpos >= 0, jnp.where(kpos < seq_len, 1, 0), 0)
            is_ctx = jnp.where(kj >= n_loc, 1, 0)
            valid = (in_band * in_range + is_ctx) > 0
            valid = jnp.concatenate([valid] * GROUP, axis=0)
        else:
            k_all, v_all = zk, zv
        outs = []
        for g in range(N_KV_HEADS):
            kg = k_all[:, g * HEAD_DIM:(g + 1) * HEAD_DIM].astype(BF16)
            vg = v_all[:, g * HEAD_DIM:(g + 1) * HEAD_DIM].astype(BF16)
            heads = [g * GROUP + h for h in range(GROUP)]
            qg = jnp.concatenate([q[:, hh * HEAD_DIM:(hh + 1) * HEAD_DIM] for hh in heads], axis=0).astype(BF16)
            s = lax.dot_general(qg, kg, (((1,), (1,)), ((), ())), preferred_element_type=F32)
            if local:
                s = jnp.where(valid, s, MASK_VALUE)
            sink = jnp.concatenate([jnp.full((rows, 1), sink_ref[hh], F32) for hh in heads], axis=0)
            m = jnp.maximum(jnp.max(s, axis=-1, keepdims=True), sink)
            p = jnp.exp(s - m)
            den = jnp.sum(p, axis=-1, keepdims=True) + jnp.exp(sink - m)
            o = jnp.dot(p.astype(BF16), vg, preferred_element_type=F32) / den
            outs += [o[h * rows:(h + 1) * rows] for h in range(GROUP)]
        o_ref[0, j * rows:(j + 1) * rows, :] = jnp.concatenate(outs, axis=1).astype(BF16)


def _attention(q, k, v, zk, zv, sink, local):
    nb, L, _ = q.shape
    C = zk.shape[1]
    n_sub = ATTN_STEP_BLOCKS if local else 1
    tq = n_sub * ATTN_BLOCK if local else L
    last_blk = L // ATTN_BLOCK - 1
    qspec = pl.BlockSpec((1, tq, D_Q), lambda b, i: (b, i, 0))
    zspec = pl.BlockSpec((1, C, D_KV), lambda b, i: (b, 0, 0))
    if local:
        prev = pl.BlockSpec((1, ATTN_BLOCK, D_KV), lambda b, i: (b, jnp.maximum(i * n_sub - 1, 0), 0))
        cur = pl.BlockSpec((1, tq, D_KV), lambda b, i: (b, i, 0))
        nxt = pl.BlockSpec((1, ATTN_BLOCK, D_KV), lambda b, i: (b, jnp.minimum((i + 1) * n_sub, last_blk), 0))
        in_specs = [qspec, prev, cur, nxt, prev, cur, nxt, zspec, zspec]
        args = (q, k, k, k, v, v, v, zk, zv)
    else:
        in_specs = [qspec, zspec, zspec]
        args = (q, zk, zv)
    return pl.pallas_call(
        functools.partial(_attn_kernel, seq_len=L, local=local),
        grid=(nb, L // tq),
        in_specs=[pl.BlockSpec(memory_space=pltpu.SMEM)] + in_specs,
        out_specs=qspec,
        out_shape=jax.ShapeDtypeStruct((nb, L, D_Q), BF16),
        compiler_params=_cparams("parallel", "arbitrary"),
        name="attention_local" if local else "attention_ctx",
    )(sink, *args)


def _mix_kernel(*refs, seq_len, aliased):
    if aliased:
        refs = refs[1:]
    (x_ref, a_ref, ap_ref, an_ref, at_ref, u_ref, up_ref, un_ref, gt_ref, g1_ref, sh2_ref, sc2_ref,
     wdw_ref, bdw_ref, lng_ref, lnb_ref, wco_ref, wao_ref, wpool_ref, psc_ref, wpo_ref, wout_ref,
     n2g_ref, rwt_ref, rb_ref, x1_ref, h2_ref, ti_ref, tg_ref, ext_ref, conv_ref, mrg_ref) = refs
    i = pl.program_id(1)
    nt = pl.num_programs(1)
    tm = x_ref.shape[1]
    has_prev = (i > 0).astype(F32)
    has_next = (i < nt - 1).astype(F32)

    def fill_ext(c_ref, p_ref, n_ref):
        ext_ref[0:HALO, :] = p_ref[0] * has_prev
        ext_ref[HALO:HALO + tm, :] = c_ref[0]
        ext_ref[HALO + tm:HALO + tm + HALO, :] = n_ref[0] * has_next

    fill_ext(a_ref, ap_ref, an_ref)
    base = HALO - CONV_K // 2
    rc = CONV_ROW_CHUNK
    for cb in range(D_CONV // LANES):
        sl = slice(cb * LANES, (cb + 1) * LANES)
        for r0 in range(0, tm, rc):
            acc = jnp.zeros((rc, LANES), F32) + bdw_ref[:, sl]
            for ph in range(SUBLANES):
                part = None
                for m in range((base + CONV_K - 1) // SUBLANES + 1):
                    j = m * SUBLANES + ph - base
                    if 0 <= j < CONV_K:
                        term = ext_ref[pl.ds(r0 + m * SUBLANES, rc + SUBLANES), sl] * wdw_ref[j:j + 1, sl]
                        part = term if part is None else part + term
                acc = acc + part[ph:ph + rc]
            conv_ref[r0:r0 + rc, sl] = acc
    acc = conv_ref[...]
    mu = jnp.mean(acc, axis=-1, keepdims=True)
    cen = acc - mu
    var = jnp.mean(cen * cen, axis=-1, keepdims=True)
    ln = cen * lax.rsqrt(var + EPS) * lng_ref[...] + lnb_ref[...]
    act = (ln * _sigmoid(ln)).astype(BF16)

    fill_ext(u_ref, up_ref, un_ref)
    t = i * tm + lax.broadcasted_iota(jnp.int32, (tm, 1), 0)
    u = u_ref[0]
    mixed = []
    for gi, w in enumerate(POOL_WINDOWS):
        sl = slice(gi * POOL_GROUP, (gi + 1) * POOL_GROUP)
        run = ext_ref[:, sl]
        span = 1
        while span < min(w, SUBLANES):
            run = run[:run.shape[0] - span] + run[span:]
            span *= 2
        lo = HALO - w // 2
        tot = run[lo:lo + tm]
        for extra in range(span, w, span):
            tot = tot + run[lo + extra:lo + extra + tm]
        cnt = (jnp.minimum(t + (w - w // 2), seq_len) - jnp.maximum(t - w // 2, 0)).astype(F32)
        pooled = tot / cnt - u[:, sl]
        mixed.append(jnp.dot(pooled.astype(BF16), wpool_ref[gi], preferred_element_type=F32))
    mixed = (jnp.concatenate(mixed, axis=1) * psc_ref[...]).astype(BF16)
    attn = at_ref[0]

    for c0 in range(0, D_MODEL, MERGE_COLS):
        cs = slice(c0, c0 + MERGE_COLS)
        branches = ((act, wco_ref), (attn, wao_ref), (mixed, wpo_ref))
        m = None
        for j, (lhs, w_ref) in enumerate(branches):
            gate = gt_ref[0, :, j * D_MODEL + c0:j * D_MODEL + c0 + MERGE_COLS].astype(F32)
            term = gate * jnp.dot(lhs, w_ref[:, cs], preferred_element_type=F32)
            m = term if m is None else m + term
        mrg_ref[:, cs] = m.astype(BF16)
    x1 = x_ref[0] + g1_ref[0] * jnp.dot(mrg_ref[...], wout_ref[...], preferred_element_type=F32)
    x1_ref[0] = x1

    h2 = _rms(x1, n2g_ref[...]) * (1.0 + sc2_ref[0]) + sh2_ref[0]
    _store_token_rows(h2_ref, (), h2)

    logits = lax.dot_general(rwt_ref[...], h2, (((1,), (1,)), ((), ())), preferred_element_type=F32,
                             precision=lax.Precision.HIGHEST) + rb_ref[...]
    eid = lax.broadcasted_iota(jnp.int32, logits.shape, 0)
    vals = []
    for k in range(TOP_K):
        mx = jnp.max(logits, axis=0, keepdims=True)
        idx = jnp.min(jnp.where(logits == mx, eid, N_EXPERTS), axis=0, keepdims=True)
        ti_ref[0, k:k + 1, :] = idx
        vals.append(mx)
        logits = jnp.where(eid == idx, -jnp.inf, logits)
    es = [jnp.exp(v - vals[0]) for v in vals]
    den = es[0] + es[1] + es[2] + es[3]
    for k in range(TOP_K):
        tg_ref[0, k:k + 1, :] = es[k] / den


def _mix(x, a, attn, u, gates, g1, sh2, sc2, p, h_all, row_off, total_rows):
    nb, L, _ = x.shape
    tm = min(TOKEN_TILE, L)
    nh = tm // HALO
    last_h = L // HALO - 1
    per = L // tm
    tile = lambda w: pl.BlockSpec((1, tm, w), lambda b, i: (b, i, 0))
    prev = pl.BlockSpec((1, HALO, 512), lambda b, i: (b, jnp.maximum(i * nh - 1, 0), 0))
    nxt = pl.BlockSpec((1, HALO, 512), lambda b, i: (b, jnp.minimum((i + 1) * nh, last_h), 0))
    per_b = pl.BlockSpec((1, 1, D_MODEL), lambda b, i: (b, 0, 0))
    full = lambda arr: pl.BlockSpec(arr.shape, lambda b, i: (0,) * arr.ndim)
    route = pl.BlockSpec((1, TOP_K, tm), lambda b, i: (b, 0, i))
    weights = [p["w_dw"], p["b_dw"], p["ln_g"], p["ln_b"], p["w_conv_out"], p["w_attn_out"], p["w_pool"],
               p["pool_scale"], p["w_pool_out"], p["w_out"], p["norm2_g"], p["router_wt"], p["router_b"]]
    in_specs = [tile(D_MODEL), tile(D_CONV), prev, nxt, tile(D_Q), tile(D_POOL), prev, nxt,
                tile(N_BRANCH * D_MODEL), per_b, per_b, per_b] + [full(w) for w in weights]
    args = [x, a, a, a, attn, u, u, u, gates, g1, sh2, sc2, *weights]
    aliases = {}
    if h_all is not None:
        in_specs = [pl.BlockSpec(memory_space=pl.ANY)] + in_specs
        args = [h_all] + args
        aliases = {0: 1}
    return pl.pallas_call(
        functools.partial(_mix_kernel, seq_len=L, aliased=h_all is not None),
        grid=(nb, per),
        in_specs=in_specs,
        out_specs=[tile(D_MODEL),
                   pl.BlockSpec((tm * ROW_CHUNKS, LANES), lambda b, i: (row_off // tm + b * per + i, 0)),
                   route, route],
        out_shape=[jax.ShapeDtypeStruct((nb, L, D_MODEL), F32),
                   jax.ShapeDtypeStruct((total_rows * ROW_CHUNKS, LANES), F32),
                   jax.ShapeDtypeStruct((nb, TOP_K, L), jnp.int32), jax.ShapeDtypeStruct((nb, TOP_K, L), F32)],
        scratch_shapes=[pltpu.VMEM((tm + 2 * HALO, 512), F32), pltpu.VMEM((tm, D_CONV), F32),
                        pltpu.VMEM((tm, D_MODEL), BF16)],
        input_output_aliases=aliases,
        compiler_params=_cparams("parallel", "arbitrary"),
        name="mix",
    )(*args)


def _row_copy(src, src_row8, dst, dst_row8, sem):
    return pltpu.make_async_copy(src.at[pl.ds(src_row8, ROW_CHUNKS)], dst.at[pl.ds(dst_row8, ROW_CHUNKS)], sem)


def _expert_kernel(be_ref, tok_ref, ntok_ref, pslot_ref, slot_ref, h_hbm, wgu_ref, bgu_ref, wd_ref, bd_ref, y_hbm,
                   xbuf, ybuf, xs, wgu_bf, wd_bf, gsem, ssem):
    i = pl.program_id(0)
    n = pl.num_programs(0)
    cur = i % 2
    rows = xs.shape[0]

    def gather(ids_ref, s, wait):
        for r in range(rows):
            src = 0 if wait else pl.multiple_of(ids_ref[0, 0, r], ROW_CHUNKS)
            cp = _row_copy(h_hbm, src, xbuf.at[s], r * ROW_CHUNKS, gsem.at[s])
            cp.wait() if wait else cp.start(priority=r % DMA_THREADS)

    def scatter(ids_ref, s, wait):
        for r in range(rows):
            dst = 0 if wait else pl.multiple_of(ids_ref[0, 0, r], ROW_CHUNKS)
            cp = _row_copy(ybuf.at[s], r * ROW_CHUNKS, y_hbm, dst, ssem.at[s])
            cp.wait() if wait else cp.start(priority=r % DMA_THREADS)

    @pl.when(i == 0)
    def _():
        gather(tok_ref, 0, False)
        ybuf[1] = jnp.zeros(ybuf.shape[1:], F32)

    @pl.when(i > 0)
    def _():
        scatter(None, cur, True)

    @pl.when(jnp.logical_or(i == 0, be_ref[i] != be_ref[jnp.maximum(i - 1, 0)]))
    def _():
        wgu_bf[...] = wgu_ref[0, 0].astype(BF16)
        wd_bf[...] = wd_ref[0, 0].astype(BF16)

    gather(None, cur, True)
    xs[...] = _load_token_rows(xbuf, (cur,), rows).astype(BF16)
    gather(ntok_ref, 1 - cur, False)
    scatter(pslot_ref, 1 - cur, False)
    gu = jnp.dot(xs[...], wgu_bf[...], preferred_element_type=F32) + bgu_ref[0, 0]
    g = jnp.minimum(gu[:, :D_EXPERT], SWIGLU_LIMIT)
    up = jnp.clip(gu[:, D_EXPERT:], -SWIGLU_LIMIT, SWIGLU_LIMIT)
    act = (up + 1.0) * (g * _sigmoid(SWIGLU_ALPHA * g))
    y = jnp.dot(act.astype(BF16), wd_bf[...], preferred_element_type=F32) + bd_ref[0, 0]
    _store_token_rows(ybuf, (cur,), y)

    @pl.when(i == n - 1)
    def _():
        scatter(None, 1 - cur, True)
        scatter(slot_ref, cur, False)
        scatter(None, cur, True)
        gather(None, 1 - cur, True)


def _experts(h_flat, block_e, row_tok, row_slot, layer, w_gu, b_gu, w_down, b_down):
    n_blk = block_e.shape[0]
    rows = MOE_ROWS
    P = n_blk * rows
    smem_blk = lambda f: pl.BlockSpec((1, 1, rows), f, memory_space=pltpu.SMEM)
    grid_spec = pltpu.PrefetchScalarGridSpec(
        num_scalar_prefetch=1,
        grid=(n_blk,),
        in_specs=[smem_blk(lambda i, be: (i, 0, 0)),
                  smem_blk(lambda i, be: (jnp.minimum(i + 1, n_blk - 1), 0, 0)),
                  smem_blk(lambda i, be: (i, 0, 0)),
                  smem_blk(lambda i, be: (i + 1, 0, 0)),
                  pl.BlockSpec(memory_space=pl.ANY),
                  pl.BlockSpec((1, 1, D_MODEL, 2 * D_EXPERT), lambda i, be: (layer, be[i], 0, 0)),
                  pl.BlockSpec((1, 1, 1, 2 * D_EXPERT), lambda i, be: (layer, be[i], 0, 0)),
                  pl.BlockSpec((1, 1, D_EXPERT, D_MODEL), lambda i, be: (layer, be[i], 0, 0)),
                  pl.BlockSpec((1, 1, 1, D_MODEL), lambda i, be: (layer, be[i], 0, 0))],
        out_specs=pl.BlockSpec(memory_space=pl.ANY),
        scratch_shapes=[pltpu.VMEM((2, rows * ROW_CHUNKS, LANES), F32),
                        pltpu.VMEM((2, rows * ROW_CHUNKS, LANES), F32),
                        pltpu.VMEM((rows, D_MODEL), BF16), pltpu.VMEM((D_MODEL, 2 * D_EXPERT), BF16), pltpu.VMEM((D_EXPERT, D_MODEL), BF16),
                        pltpu.SemaphoreType.DMA((2,)), pltpu.SemaphoreType.DMA((2,))],
    )
    tok = (row_tok * ROW_CHUNKS).reshape(n_blk, 1, rows)
    slots = (jnp.concatenate([P + jnp.arange(rows, dtype=jnp.int32), row_slot]) * ROW_CHUNKS
             ).reshape(n_blk + 1, 1, rows)
    return pl.pallas_call(
        _expert_kernel,
        grid_spec=grid_spec,
        out_shape=jax.ShapeDtypeStruct(((P + rows) * ROW_CHUNKS, LANES), F32),
        compiler_params=_cparams("arbitrary"),
        name="experts",
    )(block_e, tok, tok, slots, slots, h_flat, w_gu, b_gu[:, :, None, :], w_down, b_down[:, :, None, :])


def _route(ti_flat, tok_of_slot):
    A = ti_flat.shape[0]
    rows = MOE_ROWS
    n_blk = -(-A // rows) + N_EXPERTS
    P = n_blk * rows
    slot_bits = max(A - 1, 1).bit_length()
    keys = jnp.sort(ti_flat * (1 << slot_bits) + jnp.arange(A, dtype=jnp.int32))
    order = keys & ((1 << slot_bits) - 1)
    experts = jnp.arange(N_EXPERTS, dtype=jnp.int32)
    counts = jnp.sum((ti_flat[:, None] == experts[None, :]).astype(jnp.int32), axis=0)
    padded = (counts + rows - 1) // rows * rows
    start = jnp.cumsum(counts) - counts
    padded_end = jnp.cumsum(padded)
    padded_start = padded_end - padded
    blk_row0 = jnp.arange(n_blk, dtype=jnp.int32) * rows
    block_e = jnp.minimum(jnp.sum((padded_end[None, :] <= blk_row0[:, None]).astype(jnp.int32), axis=1),
                          N_EXPERTS - 1).astype(jnp.int32)
    r = jnp.arange(P, dtype=jnp.int32)
    e_of_r = jnp.repeat(block_e, rows)
    j = r - padded_start[e_of_r]
    valid = (j >= 0) & (j < counts[e_of_r])
    slot = order[jnp.clip(start[e_of_r] + j, 0, A - 1)]
    pad_rank = jnp.cumsum(jnp.where(valid, 0, 1).astype(jnp.int32)) - 1
    row_slot = jnp.where(valid, slot, A + pad_rank).astype(jnp.int32)
    row_tok = jnp.where(valid, tok_of_slot[slot], 0).astype(jnp.int32)
    return block_e, row_tok, row_slot


def _slot_tokens(nb, L, row_off):
    b = jnp.arange(nb, dtype=jnp.int32)[:, None, None]
    t = jnp.arange(L, dtype=jnp.int32)[None, None, :]
    return jnp.broadcast_to(row_off + b * L + t, (nb, TOP_K, L)).reshape(-1)


def _final_kernel(x_ref, tg_ref, g2_ref, y0, y1, y2, y3, fg_ref, o_ref):
    o_ref[0] = _rms(_combine_rows(x_ref[0], g2_ref[0], tg_ref[0], (y0, y1, y2, y3)), fg_ref[...])


def _final(x1, y, tg, g2, final_g):
    nb, L, _ = x1.shape
    tm = min(TOKEN_TILE, L)
    tile = lambda w: pl.BlockSpec((1, tm, w), lambda b, i: (b, i, 0))
    return pl.pallas_call(
        _final_kernel,
        grid=(nb, L // tm),
        in_specs=[tile(D_MODEL), tile(TOP_K), pl.BlockSpec((1, 1, D_MODEL), lambda b, i: (b, 0, 0))]
        + _y_specs(tm, L, 0) + [pl.BlockSpec((1, D_MODEL), lambda b, i: (0, 0))],
        out_specs=tile(D_MODEL),
        out_shape=jax.ShapeDtypeStruct((nb, L, D_MODEL), F32),
        compiler_params=_cparams("parallel", "arbitrary"),
        name="final_combine_norm",
    )(x1, tg, g2, y, y, y, y, final_g.reshape(1, D_MODEL))


def _rope_tables(seq_len):
    rows = seq_len // GRID_W
    row = jnp.repeat(jnp.arange(rows), GRID_W).astype(F32)
    col = jnp.tile(jnp.arange(GRID_W), rows).astype(F32)
    inv = ROPE_THETA ** (-jnp.arange(0, ROPE_AXIS_DIM, 2, dtype=F32) / ROPE_AXIS_DIM)
    ang = jnp.concatenate([row[:, None] * inv, col[:, None] * inv], axis=-1)
    cos, sin = jnp.cos(ang), jnp.sin(ang)
    cos_t = jnp.concatenate([cos, cos, cos, cos], axis=-1)
    sin_t = jnp.concatenate([-sin, sin, -sin, sin], axis=-1)
    return cos_t, sin_t


def kernel(x, c, ctx, c_ctx, w_mod, b_mod, norm1_g, w_in, b_gate, w_dw, b_dw, conv_ln_g, conv_ln_b, w_conv_out,
           attn_sink, w_attn_out, w_pool, pool_scale, w_pool_out, w_out, norm2_g, router_w, router_b, w_gu, b_gu,
           w_down, b_down, final_g):
    B, S, D = x.shape
    C = ctx.shape[1]
    T, TZ = B * S, B * C
    depth = w_mod.shape[0]
    cos_x, sin_x = _rope_tables(S)
    cos_z = jnp.ones((C, 2 * HEAD_DIM), F32)
    sin_z = jnp.zeros((C, 2 * HEAD_DIM), F32)
    c_rows = jnp.zeros((16, D), F32).at[:B].set(c).at[B].set(c_ctx)
    z = ctx
    comb_x = comb_z = None
    for l in range(depth):
        last = l == depth - 1
        mod = _modulation(c_rows, w_mod[l], b_mod[l])
        mx = mod[:B].reshape(B, 1, 6, D)
        sh1, sc1, g1, sh2, sc2, g2 = [mx[:, :, j] for j in range(6)]
        mz = jnp.broadcast_to(mod[B].reshape(1, 1, 6, D), (B, 1, 6, D))
        zsh1, zsc1, zg1, zsh2, zsc2, zg2 = [mz[:, :, j] for j in range(6)]
        w_in_bf = w_in[l].astype(BF16)
        p = dict(w_dw=w_dw[l], b_dw=b_dw[l].reshape(1, -1), ln_g=conv_ln_g[l].reshape(1, -1),
                 ln_b=conv_ln_b[l].reshape(1, -1), w_conv_out=w_conv_out[l].astype(BF16),
                 w_attn_out=w_attn_out[l].astype(BF16), w_pool=w_pool[l].astype(BF16),
                 pool_scale=pool_scale[l].reshape(1, -1), w_pool_out=w_pool_out[l].astype(BF16),
                 w_out=w_out[l].astype(BF16), norm2_g=norm2_g[l].reshape(1, -1), router_wt=router_w[l].T,
                 router_b=router_b[l].reshape(-1, 1))

        res = _inproj(x, comb_x, sh1, sc1, norm1_g[l], w_in_bf, b_gate[l], cos_x, sin_x)
        a, q, k, v, u, gates = res[:6]
        x = res[6] if comb_x is not None else x
        zres = _inproj(z, comb_z, zsh1, zsc1, norm1_g[l], w_in_bf, b_gate[l], cos_z, sin_z)
        za, zq, zk, zv, zu, zgates = zres[:6]
        z = zres[6] if comb_z is not None else z

        attn = _attention(q, k, v, zk, zv, attn_sink[l], local=True)
        rows_all = T if last else T + TZ
        x1, h_all, ti, tg = _mix(x, a, attn, u, gates, g1, sh2, sc2, p, None, 0, rows_all)
        ti_flat, tok_of_slot = ti.reshape(-1), _slot_tokens(B, S, 0)
        if not last:
            zattn = _attention(zq, None, None, zk, zv, attn_sink[l], local=False)
            z1, h_all, zti, ztg = _mix(z, za, zattn, zu, zgates, zg1, zsh2, zsc2, p, h_all, T, rows_all)
            ti_flat = jnp.concatenate([ti_flat, zti.reshape(-1)])
            tok_of_slot = jnp.concatenate([tok_of_slot, _slot_tokens(B, C, T)])
        block_e, row_tok, row_slot = _route(ti_flat, tok_of_slot)
        y = _experts(h_all, block_e, row_tok, row_slot, l, w_gu, b_gu, w_down, b_down)
        comb_x = (y, jnp.swapaxes(tg, 1, 2), g2, 0)
        x = x1
        if not last:
            comb_z = (y, jnp.swapaxes(ztg, 1, 2), zg2, T * TOP_K)
            z = z1
    y, tg_rows, g2, _ = comb_x
    return _final(x, y, tg_rows, g2, final_g)
```
